```python
import jax, jax.numpy as jnp
from jax import lax
import numpy as np

D_MODEL = 1024
BATCH = 8
SEQ = 2048
DEPTH = 4
DEC_BATCH = 128
DEC_SEQ = 1
PAST_LEN = 16384
PAGE_SIZE = 128

D_POOL = D_MODEL // 2
N_POOL_GROUPS = 4
POOL_GROUP_DIM = D_POOL // N_POOL_GROUPS
POOL_WINDOWS = (2, 4, 8, 16)
POOL_BUF = max(POOL_WINDOWS) - 1
D_SGU = D_MODEL // 2
SGU_HEADS = 4
SGU_HEAD_DIM = D_SGU // SGU_HEADS
CHUNK = 128
D_MIX = D_POOL + D_SGU
D_IN = D_POOL + 2 * D_SGU
N_GROUPS = 4
EXPERTS_PER_GROUP = 8
N_EXPERTS = N_GROUPS * EXPERTS_PER_GROUP
TOP_K_INNER = 2
D_EXPERT = 512
DISPATCH_BLOCK = 128
DEEPNORM_ALPHA = (2.0 * DEPTH) ** 0.25
DEEPNORM_BETA = (8.0 * DEPTH) ** -0.25
LN_EPS = 1e-5

kernel_name = "hymba_pool_sgu_hmoe_deepnorm_adaln_step"


def layer_norm(x, g, b):
    xf = x.astype(jnp.float32)
    mu = jnp.mean(xf, axis=-1, keepdims=True)
    var = jnp.mean(jnp.square(xf - mu), axis=-1, keepdims=True)
    y = (xf - mu) * lax.rsqrt(var + LN_EPS)
    return (y * g.astype(jnp.float32) + b.astype(jnp.float32)).astype(x.dtype)


def pool_mixer(p_ext, pos0, pool_w, pool_scale):
    B, Lx, _ = p_ext.shape
    L = Lx - POOL_BUF
    cs = jnp.cumsum(p_ext.astype(jnp.float32), axis=1)
    cs0 = jnp.concatenate([jnp.zeros((B, 1, D_POOL), jnp.float32), cs], axis=1)
    end = POOL_BUF + 1 + jnp.arange(L)
    pos = pos0 + jnp.arange(L)
    means = []
    for g, w in enumerate(POOL_WINDOWS):
        lo, hi = g * POOL_GROUP_DIM, (g + 1) * POOL_GROUP_DIM
        s = cs0[:, end, lo:hi] - cs0[:, end - w, lo:hi]
        cnt = jnp.minimum(pos + 1, w).astype(jnp.float32)
        means.append(s / cnt[None, :, None])
    mean = jnp.concatenate(means, axis=-1)
    d = (mean - p_ext[:, POOL_BUF:].astype(jnp.float32)).astype(p_ext.dtype)
    d = d.reshape(B, L, N_POOL_GROUPS, POOL_GROUP_DIM)
    out = jnp.einsum('blgc,gcd->blgd', d, pool_w).reshape(B, L, D_POOL)
    return out * pool_scale


def head_norm(v, g):
    B, L, _ = v.shape
    vf = v.reshape(B, L, SGU_HEADS, SGU_HEAD_DIM).astype(jnp.float32)
    mu = jnp.mean(vf, axis=-1, keepdims=True)
    var = jnp.mean(jnp.square(vf - mu), axis=-1, keepdims=True)
    vn = ((vf - mu) * lax.rsqrt(var + LN_EPS)).reshape(B, L, D_SGU)
    return (vn * g.astype(jnp.float32)).astype(v.dtype)


def spatial_gating(u, vn, sgu_w, sgu_b):
    B, L, _ = u.shape
    n = -(-L // CHUNK)
    pad = n * CHUNK - L
    vp = jnp.pad(vn, ((0, 0), (0, pad), (0, 0))).reshape(B, n, CHUNK, SGU_HEADS, SGU_HEAD_DIM)
    mask = jnp.tril(jnp.ones((CHUNK, CHUNK), dtype=bool))
    ws = jnp.where(mask[None], sgu_w, jnp.zeros_like(sgu_w))
    z = jnp.einsum('hts,bnshc->bnthc', ws, vp) + jnp.transpose(sgu_b)[None, None, :, :, None]
    z = z.reshape(B, n * CHUNK, D_SGU)[:, :L]
    return u * z


def expert_mlp(x, wg, wu, wd):
    return (jax.nn.silu(x @ wg) * (x @ wu)) @ wd


def routed_experts(h, expert_idx, weights, w_gate, w_up, w_down):
    T, D = h.shape
    A = T * TOP_K_INNER
    e_flat = expert_idx.reshape(A)
    tok = jnp.repeat(jnp.arange(T, dtype=jnp.int32), TOP_K_INNER)
    wt = weights.reshape(A).astype(h.dtype)
    order = jnp.argsort(e_flat)
    e_s, tok_s, wt_s = e_flat[order], tok[order], wt[order]
    counts = jnp.zeros((N_EXPERTS,), jnp.int32).at[e_flat].add(1)
    padded = ((counts + DISPATCH_BLOCK - 1) // DISPATCH_BLOCK) * DISPATCH_BLOCK
    starts = jnp.cumsum(counts) - counts
    pends = jnp.cumsum(padded)
    pstarts = pends - padded
    dest = pstarts[e_s] + (jnp.arange(A, dtype=jnp.int32) - starts[e_s])
    n_blocks = -(-A // DISPATCH_BLOCK) + N_EXPERTS
    slot_tok = jnp.full((n_blocks * DISPATCH_BLOCK,), T, jnp.int32).at[dest].set(tok_s)
    block_start = jnp.arange(n_blocks, dtype=jnp.int32) * DISPATCH_BLOCK
    block_expert = jnp.clip(jnp.searchsorted(pends, block_start, side='right'), 0, N_EXPERTS - 1)
    h_pad = jnp.concatenate([h, jnp.zeros((1, D), h.dtype)], axis=0)
    xb = h_pad[slot_tok].reshape(n_blocks, DISPATCH_BLOCK, D)

    def one_block(args):
        xblk, e = args
        return expert_mlp(xblk, w_gate[e], w_up[e], w_down[e])

    yb = lax.map(one_block, (xb, block_expert)).reshape(n_blocks * DISPATCH_BLOCK, D)
    y = yb[dest] * wt_s[:, None]
    return jax.ops.segment_sum(y, tok_s, num_segments=T)


def hierarchical_moe(h, rg_w, rg_b, re_w, re_b, w_gate, w_up, w_down):
    T = h.shape[0]
    gl = (h @ rg_w + rg_b).astype(jnp.float32)
    gp = jax.nn.softmax(gl, axis=-1)
    g_sel = jnp.argmax(gl, axis=-1).astype(jnp.int32)
    p_g = jnp.take_along_axis(gp, g_sel[:, None], axis=-1)
    el = (h @ re_w + re_b).astype(jnp.float32).reshape(T, N_GROUPS, EXPERTS_PER_GROUP)
    el_sel = jnp.take_along_axis(el, g_sel[:, None, None], axis=1)[:, 0]
    top_v, top_i = lax.top_k(el_sel, TOP_K_INNER)
    w2 = jax.nn.softmax(top_v, axis=-1) * p_g
    expert_idx = g_sel[:, None] * EXPERTS_PER_GROUP + top_i.astype(jnp.int32)
    return routed_experts(h, expert_idx, w2, w_gate, w_up, w_down)


def trunk_layer(x, c, buf, pos0, w_ada, b_ada, w_in, pool_w, pool_scale, sgu_norm_g, sgu_w, sgu_b,
                w_out, ln1_g, ln1_b, rg_w, rg_b, re_w, re_b, e_wg, e_wu, e_wd, ln2_g, ln2_b):
    B, L, D = x.shape
    mod = (jax.nn.silu(c) @ w_ada + b_ada)[:, None, :]
    sh1, sc1, g1, sh2, sc2, g2 = jnp.split(mod, 6, axis=-1)
    h = x * (1 + sc1) + sh1
    proj = h @ w_in
    p = proj[..., :D_POOL]
    u = proj[..., D_POOL:D_POOL + D_SGU]
    v = proj[..., D_POOL + D_SGU:]
    p_ext = jnp.concatenate([buf.astype(p.dtype), p], axis=1)
    a = pool_mixer(p_ext, pos0, pool_w, pool_scale)
    new_buf = p_ext[:, -POOL_BUF:]
    vn = head_norm(v, sgu_norm_g)
    gsp = spatial_gating(u, vn, sgu_w, sgu_b)
    mix = jnp.concatenate([a, gsp], axis=-1) @ w_out
    x = layer_norm(DEEPNORM_ALPHA * x + (1 + g1) * mix, ln1_g, ln1_b)
    h2 = x * (1 + sc2) + sh2
    f = hierarchical_moe(h2.reshape(B * L, D), rg_w, rg_b, re_w, re_b, e_wg, e_wu, e_wd).reshape(B, L, D)
    x = layer_norm(DEEPNORM_ALPHA * x + (1 + g2) * f, ln2_g, ln2_b)
    return x, new_buf, vn


def setup_inputs(seed: int = 0) -> dict:
    key = jax.random.key(seed)
    ks = jax.random.split(key, 32)
    nrm = lambda k, s, sc: jax.random.normal(k, s, jnp.float32) * sc
    Dm = D_MODEL
    return {
        "x_prompt": nrm(ks[0], (BATCH, SEQ, Dm), 1.0),
        "x_sample": nrm(ks[1], (DEC_BATCH, DEC_SEQ, Dm), 1.0),
        "state_pool": nrm(ks[2], (DEPTH, DEC_BATCH, POOL_BUF, D_POOL), 1.0),
        "c_prompt": nrm(ks[3], (BATCH, Dm), 1.0),
        "c_sample": nrm(ks[4], (DEC_BATCH, Dm), 1.0),
        "w_ada": nrm(ks[5], (DEPTH, Dm, 6 * Dm), 0.1 * Dm ** -0.5),
        "b_ada": nrm(ks[6], (DEPTH, 6 * Dm), 0.01),
        "w_in": nrm(ks[7], (DEPTH, Dm, D_IN), Dm ** -0.5),
        "pool_w": nrm(ks[8], (DEPTH, N_POOL_GROUPS, POOL_GROUP_DIM, POOL_GROUP_DIM), POOL_GROUP_DIM ** -0.5),
        "pool_scale": 1.0 + nrm(ks[9], (DEPTH, D_POOL), 0.1),
        "sgu_norm_g": 1.0 + nrm(ks[10], (DEPTH, D_SGU), 0.1),
        "sgu_w": nrm(ks[11], (DEPTH, SGU_HEADS, CHUNK, CHUNK), CHUNK ** -0.5),
        "sgu_b": 1.0 + nrm(ks[12], (DEPTH, SGU_HEADS, CHUNK), 0.1),
        "w_out": nrm(ks[13], (DEPTH, D_MIX, Dm), DEEPNORM_BETA * D_MIX ** -0.5),
        "ln1_g": 1.0 + nrm(ks[14], (DEPTH, Dm), 0.05),
        "ln1_b": nrm(ks[15], (DEPTH, Dm), 0.02),
        "router_g_w": nrm(ks[16], (DEPTH, Dm, N_GROUPS), Dm ** -0.5),
        "router_g_b": nrm(ks[17], (DEPTH, N_GROUPS), 0.01),
        "router_e_w": nrm(ks[18], (DEPTH, Dm, N_EXPERTS), Dm ** -0.5),
        "router_e_b": nrm(ks[19], (DEPTH, N_EXPERTS), 0.01),
        "exp_w_gate": nrm(ks[20], (DEPTH, N_EXPERTS, Dm, D_EXPERT), Dm ** -0.5),
        "exp_w_up": nrm(ks[21], (DEPTH, N_EXPERTS, Dm, D_EXPERT), Dm ** -0.5),
        "exp_w_down": nrm(ks[22], (DEPTH, N_EXPERTS, D_EXPERT, Dm), DEEPNORM_BETA * D_EXPERT ** -0.5),
        "ln2_g": 1.0 + nrm(ks[23], (DEPTH, Dm), 0.05),
        "ln2_b": nrm(ks[24], (DEPTH, Dm), 0.02),
    }


def reference(x_prompt, x_sample, state_pool, c_prompt, c_sample, w_ada, b_ada, w_in, pool_w, pool_scale,
              sgu_norm_g, sgu_w, sgu_b, w_out, ln1_g, ln1_b, router_g_w, router_g_b, router_e_w, router_e_b,
              exp_w_gate, exp_w_up, exp_w_down, ln2_g, ln2_b):
    xp, xs = x_prompt, x_sample
    pool_p, pool_s, v_s = [], [], []
    zero_buf = jnp.zeros((x_prompt.shape[0], POOL_BUF, D_POOL), x_prompt.dtype)
    for l in range(DEPTH):
        params = (w_ada[l], b_ada[l], w_in[l], pool_w[l], pool_scale[l], sgu_norm_g[l], sgu_w[l], sgu_b[l],
                  w_out[l], ln1_g[l], ln1_b[l], router_g_w[l], router_g_b[l], router_e_w[l], router_e_b[l],
                  exp_w_gate[l], exp_w_up[l], exp_w_down[l], ln2_g[l], ln2_b[l])
        xp, bp, _ = trunk_layer(xp, c_prompt, zero_buf, 0, *params)
        xs, bs, vs = trunk_layer(xs, c_sample, state_pool[l], PAST_LEN, *params)
        pool_p.append(bp)
        pool_s.append(bs)
        v_s.append(vs)
    pool_state_prompt = jnp.stack(pool_p, axis=0)
    pool_state_sample = jnp.stack(pool_s, axis=0)
    sgu_v_sample = jnp.stack(v_s, axis=0)
    return (xp, xs, pool_state_prompt, pool_state_sample, sgu_v_sample)
```

```python
import functools

import jax
import jax.numpy as jnp
from jax import lax
from jax.experimental import pallas as pl
from jax.experimental.pallas import tpu as pltpu

D_MODEL = 1024
DEPTH = 4
PAST_LEN = 16384
D_POOL = 512
N_POOL_GROUPS = 4
POOL_GROUP_DIM = 128
POOL_WINDOWS = (2, 4, 8, 16)
POOL_BUF = 15
HIST = 16
D_SGU = 512
SGU_HEADS = 4
SGU_HEAD_DIM = 128
CHUNK = 128
N_GROUPS = 4
EXPERTS_PER_GROUP = 8
N_EXPERTS = 32
D_EXPERT = 512
DEEPNORM_ALPHA = (2.0 * DEPTH) ** 0.25
LN_EPS = 1e-5

LANES = 128
ROUTE_G_LANE = N_EXPERTS
SECOND_LANE = 64

TM_MIX = 256
TM_ROW = 512
BM = 256

F32 = jnp.float32
BF16 = jnp.bfloat16


def _layer_norm(r, g, b):
    mu = jnp.mean(r, axis=-1, keepdims=True)
    rc = r - mu
    var = jnp.mean(rc * rc, axis=-1, keepdims=True)
    return rc * lax.rsqrt(var + LN_EPS) * g + b


def _head_norm(vh, g):
    mu = jnp.mean(vh, axis=-1, keepdims=True)
    vc = vh - mu
    var = jnp.mean(vc * vc, axis=-1, keepdims=True)
    return vc * lax.rsqrt(var + LN_EPS) * g


def _roll_half(row):
    return pltpu.roll(jnp.broadcast_to(row, (8, LANES)), SECOND_LANE, 1)[0:1]


def _route(logits, ltri_ref, run_ref):
    tm = logits.shape[0]
    lane = lax.broadcasted_iota(jnp.int32, (tm, LANES), 1)
    lanef = lane.astype(F32)
    neg = -jnp.inf
    is_g = (lane >= ROUTE_G_LANE) & (lane < ROUTE_G_LANE + N_GROUPS)
    glm = jnp.where(is_g, logits, neg)
    gmax = jnp.max(glm, axis=1, keepdims=True)
    g_idx = jnp.min(jnp.where(glm == gmax, lanef - ROUTE_G_LANE, 1e4), axis=1, keepdims=True)
    p_g = 1.0 / jnp.sum(jnp.exp(glm - gmax), axis=1, keepdims=True)

    in_grp = (lane < N_EXPERTS) & ((lane >> 3).astype(F32) == g_idx)
    elm = jnp.where(in_grp, logits, neg)
    m1 = jnp.max(elm, axis=1, keepdims=True)
    i1 = jnp.min(jnp.where(elm == m1, lanef, 1e4), axis=1, keepdims=True)
    elm2 = jnp.where(lanef == i1, neg, elm)
    m2 = jnp.max(elm2, axis=1, keepdims=True)
    i2 = jnp.min(jnp.where(elm2 == m2, lanef, 1e4), axis=1, keepdims=True)
    e21 = jnp.exp(m2 - m1)
    den = 1.0 + e21
    w0 = (1.0 / den) * p_g
    w1 = (e21 / den) * p_g

    hit0 = lanef == i1
    hit1 = lanef == i2 + SECOND_LANE
    oh = jnp.where(hit0 | hit1, 1.0, 0.0)
    before = jnp.dot(ltri_ref[...], oh.astype(BF16), preferred_element_type=F32)
    tot = jnp.sum(oh, axis=0, keepdims=True)
    lane1 = lax.broadcasted_iota(jnp.int32, (1, LANES), 1)
    tot0 = jnp.where(lane1 < N_EXPERTS, tot, 0.0)
    tot1 = jnp.where(lane1 >= SECOND_LANE, tot, 0.0)
    run = run_ref[...]
    base = run + _roll_half(run + tot0)
    val = before + base
    rank0 = jnp.sum(jnp.where(hit0, val, 0.0), axis=1, keepdims=True)
    rank1 = jnp.sum(jnp.where(hit1, val, 0.0), axis=1, keepdims=True)
    run_ref[...] = run + tot0 + _roll_half(tot1)

    out = jnp.where(lane == 0, i1, 0.0)
    out = jnp.where(lane == 1, i2, out)
    out = jnp.where(lane == 2, w0, out)
    out = jnp.where(lane == 3, w1, out)
    out = jnp.where(lane == 4, rank0, out)
    out = jnp.where(lane == 5, rank1, out)
    return out


def _finish_mix(x, mix, g1, sh2, sc2, ln1g_ref, ln1b_ref, wr_ref, br_ref, ltri_ref, run_ref,
                x1_ref, h2_ref, route_ref):
    r = DEEPNORM_ALPHA * x + (1.0 + g1) * mix
    x1 = _layer_norm(r, ln1g_ref[...], ln1b_ref[...])
    x1_ref[...] = x1
    h2 = x1 * (1.0 + sc2) + sh2
    h2_ref[...] = h2
    logits = jnp.dot(h2, wr_ref[...], precision=lax.Precision.HIGHEST,
                     preferred_element_type=F32) + br_ref[...]
    route_ref[...] = _route(logits, ltri_ref, run_ref)


def _masked_sgu_w(sguw_ref, hd):
    t = lax.broadcasted_iota(jnp.int32, (CHUNK, CHUNK), 0)
    s = lax.broadcasted_iota(jnp.int32, (CHUNK, CHUNK), 1)
    return jnp.where(s <= t, sguw_ref[hd], 0.0).astype(BF16)


def _mod_kernel(c_ref, w_ref, b_ref, o_ref):
    s = jax.nn.silu(c_ref[...])
    o_ref[...] = jnp.dot(s.astype(BF16), w_ref[...].astype(BF16),
                         preferred_element_type=F32) + b_ref[...]


def _mod_call(c_all, w_ada, b_ada):
    n = c_all.shape[0]
    tn = 1024
    return pl.pallas_call(
        _mod_kernel,
        grid=(DEPTH, 6 * D_MODEL // tn),
        in_specs=[
            pl.BlockSpec((n, D_MODEL), lambda l, j: (0, 0)),
            pl.BlockSpec((None, D_MODEL, tn), lambda l, j: (l, 0, j)),
            pl.BlockSpec((None, 1, tn), lambda l, j: (l, 0, j)),
        ],
        out_specs=pl.BlockSpec((None, n, tn), lambda l, j: (l, 0, j)),
        out_shape=jax.ShapeDtypeStruct((DEPTH, n, 6 * D_MODEL), F32),
        name="adaln_mod",
        compiler_params=pltpu.CompilerParams(
            dimension_semantics=("arbitrary", "arbitrary")),
    )(c_all, w_ada, b_ada.reshape(DEPTH, 1, 6 * D_MODEL))


def _mix_prompt_kernel(x_ref, mod_ref, win_ref, wout_ref, poolw_ref, pscale_ref, sgug_ref,
                       sguw_ref, sgub_ref, ln1g_ref, ln1b_ref, wr_ref, br_ref, ltri_ref,
                       x1_ref, h2_ref, route_ref, nbuf_ref, cnt_ref,
                       pe_ref, mixin_ref, run_ref, *, tm, tiles_per_seq):
    b = pl.program_id(0)
    j = pl.program_id(1)

    @pl.when((b == 0) & (j == 0))
    def _():
        run_ref[...] = jnp.zeros_like(run_ref)

    @pl.when(j == 0)
    def _():
        pe_ref[0:HIST, :] = jnp.zeros((HIST, D_POOL), F32)

    x = x_ref[...]
    mod = mod_ref[...]
    sh1, sc1, g1, sh2, sc2 = (mod[i:i + 1] for i in range(5))
    h = x * (1.0 + sc1) + sh1
    proj = jnp.dot(h.astype(BF16), win_ref[...], preferred_element_type=F32)
    p = proj[:, :D_POOL]
    u = proj[:, D_POOL:D_POOL + D_SGU]
    v = proj[:, D_POOL + D_SGU:]

    pe_ref[HIST:HIST + tm, :] = p
    pos = j * tm + lax.broadcasted_iota(jnp.int32, (tm, 1), 0)
    for g, w in enumerate(POOL_WINDOWS):
        lo, hi = g * POOL_GROUP_DIM, (g + 1) * POOL_GROUP_DIM
        pg = p[:, lo:hi]
        s = pg
        for k in range(1, w):
            s = s + pe_ref[pl.ds(HIST - k, tm), lo:hi]
        cnt = jnp.minimum(pos + 1, w).astype(F32)
        d = s / cnt - pg
        a = jnp.dot(d.astype(BF16), poolw_ref[g], preferred_element_type=F32)
        mixin_ref[:, lo:hi] = (a * pscale_ref[:, lo:hi]).astype(BF16)

    @pl.when(j == tiles_per_seq - 1)
    def _():
        nbuf_ref[...] = pe_ref[tm:tm + HIST, :]

    pe_ref[0:HIST, :] = pe_ref[tm:tm + HIST, :]

    for hd in range(SGU_HEADS):
        lo, hi = hd * SGU_HEAD_DIM, (hd + 1) * SGU_HEAD_DIM
        vn = _head_norm(v[:, lo:hi], sgug_ref[:, lo:hi]).astype(BF16)
        ws = _masked_sgu_w(sguw_ref, hd)
        bcol = sgub_ref[:, hd:hd + 1]
        for c in range(tm // CHUNK):
            r0, r1 = c * CHUNK, (c + 1) * CHUNK
            z = jnp.dot(ws, vn[r0:r1], preferred_element_type=F32) + bcol
            mixin_ref[r0:r1, D_POOL + lo:D_POOL + hi] = (u[r0:r1, lo:hi] * z).astype(BF16)

    mix = jnp.dot(mixin_ref[...], wout_ref[...], preferred_element_type=F32)
    _finish_mix(x, mix, g1, sh2, sc2, ln1g_ref, ln1b_ref, wr_ref, br_ref, ltri_ref, run_ref,
                x1_ref, h2_ref, route_ref)
    cnt_ref[...] = run_ref[...]


def _full(shape):
    nd = len(shape)
    return pl.BlockSpec(shape, lambda *_: (0,) * nd)


def _mix_prompt_call(xp, mod_p, lw, n_seq, seq_len):
    tm = TM_MIX
    tps = seq_len // tm
    t = n_seq * seq_len
    row = lambda b, j: (b * tps + j, 0)
    return pl.pallas_call(
        functools.partial(_mix_prompt_kernel, tm=tm, tiles_per_seq=tps),
        grid=(n_seq, tps),
        in_specs=[
            pl.BlockSpec((tm, D_MODEL), row),
            pl.BlockSpec((None, 6, D_MODEL), lambda b, j: (b, 0, 0)),
            _full((D_MODEL, D_POOL + 2 * D_SGU)),
            _full((D_POOL + D_SGU, D_MODEL)),
            _full((N_POOL_GROUPS, POOL_GROUP_DIM, POOL_GROUP_DIM)),
            _full((1, D_POOL)),
            _full((1, D_SGU)),
            _full((SGU_HEADS, CHUNK, CHUNK)),
            _full((CHUNK, SGU_HEADS)),
            _full((1, D_MODEL)),
            _full((1, D_MODEL)),
            _full((D_MODEL, LANES)),
            _full((1, LANES)),
            _full((tm, tm)),
        ],
        out_specs=[
            pl.BlockSpec((tm, D_MODEL), row),
            pl.BlockSpec((tm, D_MODEL), row),
            pl.BlockSpec((tm, LANES), row),
            pl.BlockSpec((None, HIST, D_POOL), lambda b, j: (b, 0, 0)),
            pl.BlockSpec((1, LANES), lambda b, j: (0, 0)),
        ],
        out_shape=[
            jax.ShapeDtypeStruct((t, D_MODEL), F32),
            jax.ShapeDtypeStruct((t, D_MODEL), F32),
            jax.ShapeDtypeStruct((t, LANES), F32),
            jax.ShapeDtypeStruct((n_seq, HIST, D_POOL), F32),
            jax.ShapeDtypeStruct((1, LANES), F32),
        ],
        scratch_shapes=[
            pltpu.VMEM((HIST + tm, D_POOL), F32),
            pltpu.VMEM((tm, D_POOL + D_SGU), BF16),
            pltpu.VMEM((1, LANES), F32),
        ],
        name="mix_prompt",
        compiler_params=pltpu.CompilerParams(
            dimension_semantics=("arbitrary", "arbitrary"),
            vmem_limit_bytes=48 * 1024 * 1024),
    )(xp, mod_p, lw["w_in"], lw["w_out"], lw["pool_w"], lw["pool_scale"], lw["sgu_g"],
      lw["sgu_w"], lw["sgu_b_t"], lw["ln1_g"], lw["ln1_b"], lw["w_r"], lw["b_r"], lw["ltri_p"])


def _mix_sample_kernel(x_ref, mod_ref, hist_ref, win_ref, wout_ref, poolw_ref, pscale_ref,
                       sgug_ref, sguw0_ref, sgub0_ref, ln1g_ref, ln1b_ref, wr_ref, br_ref,
                       ltri_ref, cnt_in_ref,
                       x1_ref, h2_ref, route_ref, nbuf_ref, vn_ref, cnt_ref,
                       mixin_ref, run_ref):
    run_ref[...] = cnt_in_ref[...]
    x = x_ref[...]
    sh1, sc1, g1, sh2, sc2 = (mod_ref[:, i * D_MODEL:(i + 1) * D_MODEL] for i in range(5))
    h = x * (1.0 + sc1) + sh1
    proj = jnp.dot(h.astype(BF16), win_ref[...], preferred_element_type=F32)
    p = proj[:, :D_POOL]
    u = proj[:, D_POOL:D_POOL + D_SGU]
    v = proj[:, D_POOL + D_SGU:]

    for g, w in enumerate(POOL_WINDOWS):
        lo, hi = g * POOL_GROUP_DIM, (g + 1) * POOL_GROUP_DIM
        pg = p[:, lo:hi]
        s = pg
        for k in range(1, w):
            s = s + hist_ref[POOL_BUF - k, :, lo:hi]
        cnt = float(min(PAST_LEN + 1, w))
        d = s / cnt - pg
        a = jnp.dot(d.astype(BF16), poolw_ref[g], preferred_element_type=F32)
        mixin_ref[:, lo:hi] = (a * pscale_ref[:, lo:hi]).astype(BF16)

    for k in range(POOL_BUF - 1):
        nbuf_ref[k] = hist_ref[k + 1]
    nbuf_ref[POOL_BUF - 1] = p

    for hd in range(SGU_HEADS):
        lo, hi = hd * SGU_HEAD_DIM, (hd + 1) * SGU_HEAD_DIM
        vn = _head_norm(v[:, lo:hi], sgug_ref[:, lo:hi])
        vn_ref[:, lo:hi] = vn
        z = sguw0_ref[:, lo:hi].astype(BF16).astype(F32) * vn.astype(BF16).astype(F32) \
            + sgub0_ref[:, lo:hi]
        mixin_ref[:, D_POOL + lo:D_POOL + hi] = (u[:, lo:hi] * z).astype(BF16)

    mix = jnp.dot(mixin_ref[...], wout_ref[...], preferred_element_type=F32)
    _finish_mix(x, mix, g1, sh2, sc2, ln1g_ref, ln1b_ref, wr_ref, br_ref, ltri_ref, run_ref,
                x1_ref, h2_ref, route_ref)
    cnt_ref[...] = run_ref[...]


def _mix_sample_call(xs, mod_s, hist, lw, cnt_in):
    n = xs.shape[0]
    return pl.pallas_call(
        _mix_sample_kernel,
        grid=(1,),
        in_specs=[
            _full((n, D_MODEL)),
            _full((n, 6 * D_MODEL)),
            _full((POOL_BUF, n, D_POOL)),
            _full((D_MODEL, D_POOL + 2 * D_SGU)),
            _full((D_POOL + D_SGU, D_MODEL)),
            _full((N_POOL_GROUPS, POOL_GROUP_DIM, POOL_GROUP_DIM)),
            _full((1, D_POOL)),
            _full((1, D_SGU)),
            _full((1, D_SGU)),
            _full((1, D_SGU)),
            _full((1, D_MODEL)),
            _full((1, D_MODEL)),
            _full((D_MODEL, LANES)),
            _full((1, LANES)),
            _full((n, n)),
            _full((1, LANES)),
        ],
        out_specs=[
            _full((n, D_MODEL)),
            _full((n, D_MODEL)),
            _full((n, LANES)),
            _full((POOL_BUF, n, D_POOL)),
            _full((n, D_SGU)),
            _full((1, LANES)),
        ],
        out_shape=[
            jax.ShapeDtypeStruct((n, D_MODEL), F32),
            jax.ShapeDtypeStruct((n, D_MODEL), F32),
            jax.ShapeDtypeStruct((n, LANES), F32),
            jax.ShapeDtypeStruct((POOL_BUF, n, D_POOL), F32),
            jax.ShapeDtypeStruct((n, D_SGU), F32),
            jax.ShapeDtypeStruct((1, LANES), F32),
        ],
        scratch_shapes=[
            pltpu.VMEM((n, D_POOL + D_SGU), BF16),
            pltpu.VMEM((1, LANES), F32),
        ],
        name="mix_sample",
        compiler_params=pltpu.CompilerParams(
            dimension_semantics=("arbitrary",),
            vmem_limit_bytes=48 * 1024 * 1024),
    )(xs, mod_s, hist, lw["w_in"], lw["w_out"], lw["pool_w"], lw["pool_scale"], lw["sgu_g"],
      lw["sgu_w0"], lw["sgu_b0"], lw["ln1_g"], lw["ln1_b"], lw["w_r"], lw["b_r"], lw["ltri_s"],
      cnt_in)


def _row_copy(src_ref, src_row, dst_ref, dst_row, sem):
    return pltpu.make_async_copy(src_ref.at[pl.ds(src_row, 1), :],
                                 dst_ref.at[pl.ds(dst_row, 1), :], sem)


def _dispatch_kernel(dest_ref, h2_ref, xs_in_ref, xs_ref, sem, *, tm):
    del xs_in_ref

    def start(r, carry):
        for k in range(2):
            _row_copy(h2_ref, r, xs_ref, dest_ref[0, 2 * r + k], sem).start()
        return carry

    lax.fori_loop(0, tm, start, 0)

    def wait(r, carry):
        for k in range(2):
            _row_copy(h2_ref, r, xs_ref, dest_ref[0, 2 * r + k], sem).wait()
        return carry

    lax.fori_loop(0, tm, wait, 0)


def _dispatch_call(dest, h2, xs, tm):
    t = h2.shape[0]
    nt = t // tm
    return pl.pallas_call(
        functools.partial(_dispatch_kernel, tm=tm),
        grid=(nt,),
        in_specs=[
            pl.BlockSpec((None, 1, 2 * tm), lambda i: (i, 0, 0), memory_space=pltpu.SMEM),
            pl.BlockSpec((tm, D_MODEL), lambda i: (i, 0)),
            pl.BlockSpec(memory_space=pl.ANY),
        ],
        out_specs=pl.BlockSpec(memory_space=pl.ANY),
        out_shape=jax.ShapeDtypeStruct(xs.shape, xs.dtype),
        scratch_shapes=[pltpu.SemaphoreType.DMA],
        input_output_aliases={2: 0},
        name="dispatch_rows",
        compiler_params=pltpu.CompilerParams(dimension_semantics=("arbitrary",)),
    )(dest.reshape(nt, 1, 2 * tm), h2, xs)


def _expert_kernel(be_ref, bfirst_ref, bvalid_ref, xs_ref, wg_ref, wu_ref, wd_ref, yb_ref,
                   wgb_ref, wub_ref, wdb_ref):
    del be_ref
    i = pl.program_id(0)

    @pl.when(bfirst_ref[i] == 1)
    def _():
        wgb_ref[...] = wg_ref[...].astype(BF16)
        wub_ref[...] = wu_ref[...].astype(BF16)
        wdb_ref[...] = wd_ref[...].astype(BF16)

    @pl.when(bvalid_ref[i] == 1)
    def _():
        x = xs_ref[...].astype(BF16)
        g = jnp.dot(x, wgb_ref[...], preferred_element_type=F32)
        u = jnp.dot(x, wub_ref[...], preferred_element_type=F32)
        a = (jax.nn.silu(g) * u).astype(BF16)
        yb_ref[...] = jnp.dot(a, wdb_ref[...], preferred_element_type=F32)

    @pl.when(bvalid_ref[i] == 0)
    def _():
        yb_ref[...] = jnp.zeros_like(yb_ref)


def _expert_call(blk_e, blk_first, blk_valid, xs, w_gate, w_up, w_down):
    n_blocks = xs.shape[0] // BM
    wmap = lambda i, be, bf, bv: (be[i], 0, 0)
    grid_spec = pltpu.PrefetchScalarGridSpec(
        num_scalar_prefetch=3,
        grid=(n_blocks,),
        in_specs=[
            pl.BlockSpec((BM, D_MODEL), lambda i, be, bf, bv: (i, 0)),
            pl.BlockSpec((None, D_MODEL, D_EXPERT), wmap),
            pl.BlockSpec((None, D_MODEL, D_EXPERT), wmap),
            pl.BlockSpec((None, D_EXPERT, D_MODEL), wmap),
        ],
        out_specs=pl.BlockSpec((BM, D_MODEL), lambda i, be, bf, bv: (i, 0)),
        scratch_shapes=[
            pltpu.VMEM((D_MODEL, D_EXPERT), BF16),
            pltpu.VMEM((D_MODEL, D_EXPERT), BF16),
            pltpu.VMEM((D_EXPERT, D_MODEL), BF16),
        ],
    )
    return pl.pallas_call(
        _expert_kernel,
        grid_spec=grid_spec,
        out_shape=jax.ShapeDtypeStruct(xs.shape, F32),
        name="expert_mlp",
        compiler_params=pltpu.CompilerParams(
            dimension_semantics=("arbitrary",),
            vmem_limit_bytes=48 * 1024 * 1024),
    )(blk_e, blk_first, blk_valid, xs, w_gate, w_up, w_down)


def _combine_kernel(dest_ref, x1_ref, route_ref, g2_ref, ln2g_ref, ln2b_ref, yb_ref,
                    out_ref, ybuf_ref, sem, *, tm):
    def start(r, carry):
        for k in range(2):
            _row_copy(yb_ref, dest_ref[0, 2 * r + k], ybuf_ref.at[k], r, sem).start()
        return carry

    lax.fori_loop(0, tm, start, 0)

    def wait(r, carry):
        for k in range(2):
            _row_copy(yb_ref, dest_ref[0, 2 * r + k], ybuf_ref.at[k], r, sem).wait()
        return carry

    lax.fori_loop(0, tm, wait, 0)

    route = route_ref[...]
    f = route[:, 2:3] * ybuf_ref[0] + route[:, 3:4] * ybuf_ref[1]
    r = DEEPNORM_ALPHA * x1_ref[...] + (1.0 + g2_ref[...]) * f
    out_ref[...] = _layer_norm(r, ln2g_ref[...], ln2b_ref[...])


def _combine_call(dest, x1, route, g2, g2_spec, ln2_g, ln2_b, yb, tm):
    t = x1.shape[0]
    nt = t // tm
    return pl.pallas_call(
        functools.partial(_combine_kernel, tm=tm),
        grid=(nt,),
        in_specs=[
            pl.BlockSpec((None, 1, 2 * tm), lambda i: (i, 0, 0), memory_space=pltpu.SMEM),
            pl.BlockSpec((tm, D_MODEL), lambda i: (i, 0)),
            pl.BlockSpec((tm, LANES), lambda i: (i, 0)),
            g2_spec,
            _full((1, D_MODEL)),
            _full((1, D_MODEL)),
            pl.BlockSpec(memory_space=pl.ANY),
        ],
        out_specs=pl.BlockSpec((tm, D_MODEL), lambda i: (i, 0)),
        out_shape=jax.ShapeDtypeStruct((t, D_MODEL), F32),
        scratch_shapes=[pltpu.VMEM((2, tm, D_MODEL), F32), pltpu.SemaphoreType.DMA],
        name="combine_ln2",
        compiler_params=pltpu.CompilerParams(dimension_semantics=("arbitrary",)),
    )(dest.reshape(nt, 1, 2 * tm), x1, route, g2, ln2_g, ln2_b, yb)


def _strict_lower(n):
    r = lax.broadcasted_iota(jnp.int32, (n, n), 0)
    c = lax.broadcasted_iota(jnp.int32, (n, n), 1)
    return (c < r).astype(BF16)


def kernel(x_prompt, x_sample, state_pool, c_prompt, c_sample, w_ada, b_ada, w_in, pool_w, pool_scale, sgu_norm_g, sgu_w, sgu_b, w_out, ln1_g, ln1_b, router_g_w, router_g_b, router_e_w, router_e_b, exp_w_gate, exp_w_up, exp_w_down, ln2_g, ln2_b):
    n_seq, seq_len, _ = x_prompt.shape
    n_dec = x_sample.shape[0]
    t_p = n_seq * seq_len
    n_assign = 2 * (t_p + n_dec)
    n_blocks = -(-n_assign // BM) + N_EXPERTS
    tiles_per_seq_row = seq_len // TM_ROW

    xp = x_prompt.reshape(t_p, D_MODEL)
    xs = x_sample.reshape(n_dec, D_MODEL)
    mod = _mod_call(jnp.concatenate([c_prompt, c_sample], axis=0), w_ada, b_ada)
    ltri_p = _strict_lower(TM_MIX)
    ltri_s = _strict_lower(n_dec)
    lane_pad = jnp.zeros((D_MODEL, LANES - N_EXPERTS - N_GROUPS), F32)

    pool_p, pool_s, v_s = [], [], []
    for l in range(DEPTH):
        lw = {
            "w_in": w_in[l].astype(BF16),
            "w_out": w_out[l].astype(BF16),
            "pool_w": pool_w[l].astype(BF16),
            "pool_scale": pool_scale[l].reshape(1, D_POOL),
            "sgu_g": sgu_norm_g[l].reshape(1, D_SGU),
            "sgu_w": sgu_w[l],
            "sgu_b_t": sgu_b[l].T,
            "sgu_w0": jnp.repeat(sgu_w[l, :, 0, 0], SGU_HEAD_DIM).reshape(1, D_SGU),
            "sgu_b0": jnp.repeat(sgu_b[l, :, 0], SGU_HEAD_DIM).reshape(1, D_SGU),
            "ln1_g": ln1_g[l].reshape(1, D_MODEL),
            "ln1_b": ln1_b[l].reshape(1, D_MODEL),
            "w_r": jnp.concatenate([router_e_w[l], router_g_w[l], lane_pad], axis=1),
            "b_r": jnp.concatenate([router_e_b[l], router_g_b[l],
                                    jnp.zeros((LANES - N_EXPERTS - N_GROUPS,), F32)]).reshape(1, LANES),
            "ltri_p": ltri_p,
            "ltri_s": ltri_s,
        }
        mod_p = mod[l, :n_seq].reshape(n_seq, 6, D_MODEL)
        mod_s = mod[l, n_seq:]

        x1p, h2p, route_p, nbuf_p, cnt_p = _mix_prompt_call(xp, mod_p, lw, n_seq, seq_len)
        hist = jnp.transpose(state_pool[l], (1, 0, 2))
        x1s, h2s, route_s, nbuf_s, vn_s, cnt = _mix_sample_call(xs, mod_s, hist, lw, cnt_p)
        pool_p.append(nbuf_p[:, HIST - POOL_BUF:])
        pool_s.append(jnp.transpose(nbuf_s, (1, 0, 2)))
        v_s.append(vn_s.reshape(n_dec, 1, D_SGU))

        counts = cnt[0, :N_EXPERTS].astype(jnp.int32)
        padded = ((counts + BM - 1) // BM) * BM
        pends = jnp.cumsum(padded)
        pstarts = pends - padded

        def dest_of(route):
            e = route[:, 0:2].astype(jnp.int32)
            rank = route[:, 4:6].astype(jnp.int32)
            return pstarts[e] + rank

        dest_p = dest_of(route_p)
        dest_s = dest_of(route_s)
        block_start = jnp.arange(n_blocks, dtype=jnp.int32) * BM
        blk_valid = (block_start < pends[-1]).astype(jnp.int32)
        blk_e = jnp.clip(jnp.searchsorted(pends, block_start, side="right"), 0, N_EXPERTS - 1)
        last_e = jnp.max(jnp.where(blk_valid == 1, blk_e, 0))
        blk_e = jnp.where(blk_valid == 1, blk_e, last_e).astype(jnp.int32)
        blk_first = ((block_start == pstarts[blk_e]) & (blk_valid == 1)).astype(jnp.int32)

        xsort = jnp.zeros((n_blocks * BM, D_MODEL), F32)
        xsort = _dispatch_call(dest_p, h2p, xsort, TM_ROW)
        xsort = _dispatch_call(dest_s, h2s, xsort, n_dec)
        yb = _expert_call(blk_e, blk_first, blk_valid, xsort,
                          exp_w_gate[l], exp_w_up[l], exp_w_down[l])

        ln2g = ln2_g[l].reshape(1, D_MODEL)
        ln2b = ln2_b[l].reshape(1, D_MODEL)
        g2_p = mod_p[:, 5:6, :]
        g2_p_spec = pl.BlockSpec((None, 1, D_MODEL), lambda i: (i // tiles_per_seq_row, 0, 0))
        xp = _combine_call(dest_p, x1p, route_p, g2_p, g2_p_spec, ln2g, ln2b, yb, TM_ROW)
        g2_s = mod_s[:, 5 * D_MODEL:]
        xs = _combine_call(dest_s, x1s, route_s, g2_s, _full((n_dec, D_MODEL)), ln2g, ln2b, yb,
                           n_dec)

    return (xp.reshape(n_seq, seq_len, D_MODEL),
            xs.reshape(n_dec, 1, D_MODEL),
            jnp.stack(pool_p, axis=0),
            jnp.stack(pool_s, axis=0),
            jnp.stack(v_s, axis=0))
```

```python
import functools

import jax
import jax.numpy as jnp
from jax import lax
from jax.experimental import pallas as pl
from jax.experimental.pallas import tpu as pltpu

D_MODEL = 1024
DEPTH = 4
PAST_LEN = 16384
D_POOL = 512
N_POOL_GROUPS = 4
POOL_GROUP_DIM = 128
POOL_WINDOWS = (2, 4, 8, 16)
POOL_BUF = 15
HIST = 16
D_SGU = 512
SGU_HEADS = 4
SGU_HEAD_DIM = 128
CHUNK = 128
N_GROUPS = 4
EXPERTS_PER_GROUP = 8
N_EXPERTS = 32
D_EXPERT = 512
DEEPNORM_ALPHA = (2.0 * DEPTH) ** 0.25
LN_EPS = 1e-5

LANES = 128
SUB = 8
assert D_MODEL == SUB * LANES
ROUTE_G_LANE = N_EXPERTS
SECOND_LANE = 64

TM_MIX = 512
TM_ROW = 128
BM = 256
MAP_STEPS = 8
ROW_BITS = 16

F32 = jnp.float32
BF16 = jnp.bfloat16
VMEM_LIMIT = 48 * 1024 * 1024


def _layer_norm(r, g, b):
    mu = jnp.mean(r, axis=-1, keepdims=True)
    rc = r - mu
    var = jnp.mean(rc * rc, axis=-1, keepdims=True)
    return rc * lax.rsqrt(var + LN_EPS) * g + b


def _head_norm(vh, g):
    mu = jnp.mean(vh, axis=-1, keepdims=True)
    vc = vh - mu
    var = jnp.mean(vc * vc, axis=-1, keepdims=True)
    return vc * lax.rsqrt(var + LN_EPS) * g


def _roll_half(row):
    return pltpu.roll(jnp.broadcast_to(row, (8, LANES)), SECOND_LANE, 1)[0:1]


def _store_row_tiles(ref, val):
    rows = val.shape[0]
    for c in range(SUB):
        ref[pl.ds(c, rows, stride=SUB), :] = val[:, c * LANES:(c + 1) * LANES]


def _load_row_tiles(ref, rows):
    return jnp.concatenate(
        [ref[pl.ds(c, rows, stride=SUB), :] for c in range(SUB)], axis=1)


def _route(logits, ltri_ref, run_ref):
    tm = logits.shape[0]
    lane = lax.broadcasted_iota(jnp.int32, (tm, LANES), 1)
    lanef = lane.astype(F32)
    neg = -jnp.inf
    is_g = (lane >= ROUTE_G_LANE) & (lane < ROUTE_G_LANE + N_GROUPS)
    glm = jnp.where(is_g, logits, neg)
    gmax = jnp.max(glm, axis=1, keepdims=True)
    g_idx = jnp.min(jnp.where(glm == gmax, lanef - ROUTE_G_LANE, 1e4), axis=1, keepdims=True)
    p_g = 1.0 / jnp.sum(jnp.exp(glm - gmax), axis=1, keepdims=True)

    in_grp = (lane < N_EXPERTS) & ((lane >> 3).astype(F32) == g_idx)
    elm = jnp.where(in_grp, logits, neg)
    m1 = jnp.max(elm, axis=1, keepdims=True)
    i1 = jnp.min(jnp.where(elm == m1, lanef, 1e4), axis=1, keepdims=True)
    elm2 = jnp.where(lanef == i1, neg, elm)
    m2 = jnp.max(elm2, axis=1, keepdims=True)
    i2 = jnp.min(jnp.where(elm2 == m2, lanef, 1e4), axis=1, keepdims=True)
    e21 = jnp.exp(m2 - m1)
    den = 1.0 + e21
    w0 = (1.0 / den) * p_g
    w1 = (e21 / den) * p_g

    hit0 = lanef == i1
    hit1 = lanef == i2 + SECOND_LANE
    oh = jnp.where(hit0 | hit1, 1.0, 0.0)
    before = jnp.dot(ltri_ref[...], oh.astype(BF16), preferred_element_type=F32)
    tot = jnp.sum(oh, axis=0, keepdims=True)
    lane1 = lax.broadcasted_iota(jnp.int32, (1, LANES), 1)
    tot0 = jnp.where(lane1 < N_EXPERTS, tot, 0.0)
    tot1 = jnp.where(lane1 >= SECOND_LANE, tot, 0.0)
    run = run_ref[...]
    base = run + _roll_half(run + tot0)
    val = before + base
    rank0 = jnp.sum(jnp.where(hit0, val, 0.0), axis=1, keepdims=True)
    rank1 = jnp.sum(jnp.where(hit1, val, 0.0), axis=1, keepdims=True)
    run_ref[...] = run + tot0 + _roll_half(tot1)

    out = jnp.where(lane == 0, i1, 0.0)
    out = jnp.where(lane == 1, i2, out)
    out = jnp.where(lane == 2, w0, out)
    out = jnp.where(lane == 3, w1, out)
    out = jnp.where(lane == 4, rank0, out)
    out = jnp.where(lane == 5, rank1, out)
    return out


def _finish_mix(x, mix, g1, sh2, sc2, ln1g_ref, ln1b_ref, wr_ref, br_ref, ltri_ref, run_ref,
                x1_ref, h2_ref, route_ref):
    r = DEEPNORM_ALPHA * x + (1.0 + g1) * mix
    x1 = _layer_norm(r, ln1g_ref[...], ln1b_ref[...])
    x1_ref[...] = x1
    h2 = x1 * (1.0 + sc2) + sh2
    _store_row_tiles(h2_ref, h2)
    hi = h2.astype(BF16)
    lo = (h2 - hi.astype(F32)).astype(BF16)
    both = jnp.dot(hi, wr_ref[...], preferred_element_type=F32)
    logits = (both[:, :LANES] + both[:, LANES:]
              + jnp.dot(lo, wr_ref[:, :LANES], preferred_element_type=F32) + br_ref[...])
    route_ref[...] = _route(logits, ltri_ref, run_ref)


def _masked_sgu_w(sguw_ref, hd):
    t = lax.broadcasted_iota(jnp.int32, (CHUNK, CHUNK), 0)
    s = lax.broadcasted_iota(jnp.int32, (CHUNK, CHUNK), 1)
    return jnp.where(s <= t, sguw_ref[hd], 0.0).astype(BF16)


def _full(shape):
    nd = len(shape)
    return pl.BlockSpec(shape, lambda *_: (0,) * nd)


_ANY = pl.BlockSpec(memory_space=pl.ANY)


def _mod_kernel(c_ref, w_ref, b_ref, o_ref):
    s = jax.nn.silu(c_ref[...])
    o_ref[...] = jnp.dot(s.astype(BF16), w_ref[...].astype(BF16),
                         preferred_element_type=F32) + b_ref[...]


def _mod_call(c_all, w_ada, b_ada):
    n = c_all.shape[0]
    tn = 1024
    return pl.pallas_call(
        _mod_kernel,
        grid=(DEPTH, 6 * D_MODEL // tn),
        in_specs=[
            pl.BlockSpec((n, D_MODEL), lambda l, j: (0, 0)),
            pl.BlockSpec((None, D_MODEL, tn), lambda l, j: (l, 0, j)),
            pl.BlockSpec((None, 1, tn), lambda l, j: (l, 0, j)),
        ],
        out_specs=pl.BlockSpec((None, n, tn), lambda l, j: (l, 0, j)),
        out_shape=jax.ShapeDtypeStruct((DEPTH, n, 6 * D_MODEL), F32),
        name="adaln_mod",
        compiler_params=pltpu.CompilerParams(
            dimension_semantics=("arbitrary", "arbitrary")),
    )(c_all, w_ada, b_ada.reshape(DEPTH, 1, 6 * D_MODEL))


def _mix_prompt_kernel(x_ref, mod_ref, win_ref, wout_ref, poolw_ref, pscale_ref, sgug_ref,
                       sguw_ref, sgub_ref, ln1g_ref, ln1b_ref, wr_ref, br_ref, ltri_ref,
                       h2_in_ref, route_in_ref,
                       x1_ref, h2_ref, route_ref, nbuf_ref, cnt_ref,
                       pe_ref, mixin_ref, run_ref, *, tm, tiles_per_seq):
    del h2_in_ref, route_in_ref
    b = pl.program_id(0)
    j = pl.program_id(1)

    @pl.when((b == 0) & (j == 0))
    def _():
        run_ref[...] = jnp.zeros_like(run_ref)

    @pl.when(j == 0)
    def _():
        pe_ref[0:HIST, :] = jnp.zeros((HIST, D_POOL), F32)

    x = x_ref[...]
    mod = mod_ref[...]
    sh1, sc1, g1, sh2, sc2 = (mod[i:i + 1] for i in range(5))
    h = x * (1.0 + sc1) + sh1
    proj = jnp.dot(h.astype(BF16), win_ref[...], preferred_element_type=F32)
    p = proj[:, :D_POOL]
    u = proj[:, D_POOL:D_POOL + D_SGU]
    v = proj[:, D_POOL + D_SGU:]

    pe_ref[HIST:HIST + tm, :] = p
    pos = j * tm + lax.broadcasted_iota(jnp.int32, (tm, 1), 0)
    for g, w in enumerate(POOL_WINDOWS):
        lo, hi = g * POOL_GROUP_DIM, (g + 1) * POOL_GROUP_DIM
        pg = p[:, lo:hi]
        s = pg
        for k in range(1, w):
            s = s + pe_ref[pl.ds(HIST - k, tm), lo:hi]
        cnt = jnp.minimum(pos + 1, w).astype(F32)
        d = s / cnt - pg
        a = jnp.dot(d.astype(BF16), poolw_ref[g], preferred_element_type=F32)
        mixin_ref[:, lo:hi] = (a * pscale_ref[:, lo:hi]).astype(BF16)

    @pl.when(j == tiles_per_seq - 1)
    def _():
        nbuf_ref[...] = pe_ref[tm:tm + HIST, :]

    pe_ref[0:HIST, :] = pe_ref[tm:tm + HIST, :]

    for hd in range(SGU_HEADS):
        lo, hi = hd * SGU_HEAD_DIM, (hd + 1) * SGU_HEAD_DIM
        vn = _head_norm(v[:, lo:hi], sgug_ref[:, lo:hi]).astype(BF16)
        ws = _masked_sgu_w(sguw_ref, hd)
        bcol = sgub_ref[:, hd:hd + 1]
        for c in range(tm // CHUNK):
            r0, r1 = c * CHUNK, (c + 1) * CHUNK
            z = jnp.dot(ws, vn[r0:r1], preferred_element_type=F32) + bcol
            mixin_ref[r0:r1, D_POOL + lo:D_POOL + hi] = (u[r0:r1, lo:hi] * z).astype(BF16)

    mix = jnp.dot(mixin_ref[...], wout_ref[...], preferred_element_type=F32)
    _finish_mix(x, mix, g1, sh2, sc2, ln1g_ref, ln1b_ref, wr_ref, br_ref, ltri_ref, run_ref,
                x1_ref, h2_ref, route_ref)
    cnt_ref[...] = run_ref[...]


def _mix_prompt_call(x_all, h2_buf, route_buf, mod_p, lw, n_seq, seq_len):
    tm = TM_MIX
    tps = seq_len // tm
    t_all = x_all.shape[0]
    row = lambda b, j: (b * tps + j, 0)
    return pl.pallas_call(
        functools.partial(_mix_prompt_kernel, tm=tm, tiles_per_seq=tps),
        grid=(n_seq, tps),
        in_specs=[
            pl.BlockSpec((tm, D_MODEL), row),
            pl.BlockSpec((None, 6, D_MODEL), lambda b, j: (b, 0, 0)),
            _full((D_MODEL, D_POOL + 2 * D_SGU)),
            _full((D_POOL + D_SGU, D_MODEL)),
            _full((N_POOL_GROUPS, POOL_GROUP_DIM, POOL_GROUP_DIM)),
            _full((1, D_POOL)),
            _full((1, D_SGU)),
            _full((SGU_HEADS, CHUNK, CHUNK)),
            _full((CHUNK, SGU_HEADS)),
            _full((1, D_MODEL)),
            _full((1, D_MODEL)),
            _full((D_MODEL, 2 * LANES)),
            _full((1, LANES)),
            _full((tm, tm)),
            _ANY, _ANY,
        ],
        out_specs=[
            pl.BlockSpec((tm, D_MODEL), row),
            pl.BlockSpec((tm * SUB, LANES), row),
            pl.BlockSpec((tm, LANES), row),
            pl.BlockSpec((None, HIST, D_POOL), lambda b, j: (b, 0, 0)),
            pl.BlockSpec((1, LANES), lambda b, j: (0, 0)),
        ],
        out_shape=[
            jax.ShapeDtypeStruct((t_all, D_MODEL), F32),
            jax.ShapeDtypeStruct((t_all * SUB, LANES), F32),
            jax.ShapeDtypeStruct((t_all, LANES), F32),
            jax.ShapeDtypeStruct((n_seq, HIST, D_POOL), F32),
            jax.ShapeDtypeStruct((1, LANES), F32),
        ],
        scratch_shapes=[
            pltpu.VMEM((HIST + tm, D_POOL), F32),
            pltpu.VMEM((tm, D_POOL + D_SGU), BF16),
            pltpu.VMEM((1, LANES), F32),
        ],
        input_output_aliases={0: 0, 14: 1, 15: 2},
        name="mix_prompt",
        compiler_params=pltpu.CompilerParams(
            dimension_semantics=("arbitrary", "arbitrary"),
            vmem_limit_bytes=VMEM_LIMIT),
    )(x_all, mod_p, lw["w_in"], lw["w_out"], lw["pool_w"], lw["pool_scale"], lw["sgu_g"],
      lw["sgu_w"], lw["sgu_b_t"], lw["ln1_g"], lw["ln1_b"], lw["w_r"], lw["b_r"], lw["ltri_p"],
      h2_buf, route_buf)


def _mix_sample_kernel(x_ref, mod_ref, hist_ref, win_ref, wout_ref, poolw_ref, pscale_ref,
                       sgug_ref, sguw0_ref, sgub0_ref, ln1g_ref, ln1b_ref, wr_ref, br_ref,
                       ltri_ref, cnt_in_ref, h2_in_ref, route_in_ref,
                       x1_ref, h2_ref, route_ref, nbuf_ref, vn_ref, cnt_ref,
                       mixin_ref, run_ref):
    del h2_in_ref, route_in_ref
    run_ref[...] = cnt_in_ref[...]
    x = x_ref[...]
    sh1, sc1, g1, sh2, sc2 = (mod_ref[:, i * D_MODEL:(i + 1) * D_MODEL] for i in range(5))
    h = x * (1.0 + sc1) + sh1
    proj = jnp.dot(h.astype(BF16), win_ref[...], preferred_element_type=F32)
    p = proj[:, :D_POOL]
    u = proj[:, D_POOL:D_POOL + D_SGU]
    v = proj[:, D_POOL + D_SGU:]

    for g, w in enumerate(POOL_WINDOWS):
        lo, hi = g * POOL_GROUP_DIM, (g + 1) * POOL_GROUP_DIM
        pg = p[:, lo:hi]
        s = pg
        for k in range(1, w):
            s = s + hist_ref[POOL_BUF - k, :, lo:hi]
        cnt = float(min(PAST_LEN + 1, w))
        d = s / cnt - pg
        a = jnp.dot(d.astype(BF16), poolw_ref[g], preferred_element_type=F32)
        mixin_ref[:, lo:hi] = (a * pscale_ref[:, lo:hi]).astype(BF16)

    for k in range(POOL_BUF - 1):
        nbuf_ref[k] = hist_ref[k + 1]
    nbuf_ref[POOL_BUF - 1] = p

    for hd in range(SGU_HEADS):
        lo, hi = hd * SGU_HEAD_DIM, (hd + 1) * SGU_HEAD_DIM
        vn = _head_norm(v[:, lo:hi], sgug_ref[:, lo:hi])
        vn_ref[:, lo:hi] = vn
        z = sguw0_ref[:, lo:hi].astype(BF16).astype(F32) * vn.astype(BF16).astype(F32) \
            + sgub0_ref[:, lo:hi]
        mixin_ref[:, D_POOL + lo:D_POOL + hi] = (u[:, lo:hi] * z).astype(BF16)

    mix = jnp.dot(mixin_ref[...], wout_ref[...], preferred_element_type=F32)
    _finish_mix(x, mix, g1, sh2, sc2, ln1g_ref, ln1b_ref, wr_ref, br_ref, ltri_ref, run_ref,
                x1_ref, h2_ref, route_ref)
    cnt_ref[...] = run_ref[...]


def _mix_sample_call(x1_all, h2_all, route_all, mod_s, hist, lw, cnt_in, n_dec, row0):
    blk = row0 // n_dec
    rows = lambda r, w: pl.BlockSpec((r, w), lambda i: (blk, 0))
    return pl.pallas_call(
        _mix_sample_kernel,
        grid=(1,),
        in_specs=[
            rows(n_dec, D_MODEL),
            _full((n_dec, 6 * D_MODEL)),
            _full((POOL_BUF, n_dec, D_POOL)),
            _full((D_MODEL, D_POOL + 2 * D_SGU)),
            _full((D_POOL + D_SGU, D_MODEL)),
            _full((N_POOL_GROUPS, POOL_GROUP_DIM, POOL_GROUP_DIM)),
            _full((1, D_POOL)),
            _full((1, D_SGU)),
            _full((1, D_SGU)),
            _full((1, D_SGU)),
            _full((1, D_MODEL)),
            _full((1, D_MODEL)),
            _full((D_MODEL, 2 * LANES)),
            _full((1, LANES)),
            _full((n_dec, n_dec)),
            _full((1, LANES)),
            _ANY, _ANY,
        ],
        out_specs=[
            rows(n_dec, D_MODEL),
            rows(n_dec * SUB, LANES),
            rows(n_dec, LANES),
            _full((POOL_BUF, n_dec, D_POOL)),
            _full((n_dec, D_SGU)),
            _full((1, LANES)),
        ],
        out_shape=[
            jax.ShapeDtypeStruct(x1_all.shape, F32),
            jax.ShapeDtypeStruct(h2_all.shape, F32),
            jax.ShapeDtypeStruct(route_all.shape, F32),
            jax.ShapeDtypeStruct((POOL_BUF, n_dec, D_POOL), F32),
            jax.ShapeDtypeStruct((n_dec, D_SGU), F32),
            jax.ShapeDtypeStruct((1, LANES), F32),
        ],
        scratch_shapes=[
            pltpu.VMEM((n_dec, D_POOL + D_SGU), BF16),
            pltpu.VMEM((1, LANES), F32),
        ],
        input_output_aliases={0: 0, 16: 1, 17: 2},
        name="mix_sample",
        compiler_params=pltpu.CompilerParams(
            dimension_semantics=("arbitrary",),
            vmem_limit_bytes=VMEM_LIMIT),
    )(x1_all, mod_s, hist, lw["w_in"], lw["w_out"], lw["pool_w"], lw["pool_scale"], lw["sgu_g"],
      lw["sgu_w0"], lw["sgu_b0"], lw["ln1_g"], lw["ln1_b"], lw["w_r"], lw["b_r"], lw["ltri_s"],
      cnt_in, h2_all, route_all)


MAP_UNROLL = 24


def _slot_map_kernel(dest_ref, fill_ref, out_ref, sem, *, chunk):
    i = pl.program_id(0)

    @pl.when(i == 0)
    def _():
        fill = pltpu.make_async_copy(fill_ref, out_ref, sem)
        fill.start()
        fill.wait()

    word_step = 1 + (1 << ROW_BITS)
    word0 = (i % MAP_STEPS) * chunk + ((i * chunk) << ROW_BITS)

    def put(jj, word):
        for k in range(MAP_UNROLL):
            out_ref[dest_ref[0, jj * MAP_UNROLL + k]] = word + k * word_step
        return word + MAP_UNROLL * word_step

    lax.fori_loop(0, chunk // MAP_UNROLL, put, word0)


def _slot_map_call(dest_cm, fill_words):
    chunk = dest_cm.shape[0] // (2 * MAP_STEPS)
    assert chunk % MAP_UNROLL == 0
    return pl.pallas_call(
        functools.partial(_slot_map_kernel, chunk=chunk),
        grid=(2 * MAP_STEPS,),
        in_specs=[pl.BlockSpec((None, 1, chunk), lambda i: (i, 0, 0),
                               memory_space=pltpu.SMEM),
                  _ANY],
        out_specs=pl.BlockSpec(memory_space=pltpu.SMEM),
        out_shape=jax.ShapeDtypeStruct(fill_words.shape, jnp.int32),
        scratch_shapes=[pltpu.SemaphoreType.DMA],
        name="slot_map",
        compiler_params=pltpu.CompilerParams(dimension_semantics=("arbitrary",)),
    )(dest_cm.reshape(2 * MAP_STEPS, 1, chunk), fill_words)


def _tile_copy(src_ref, src_row, dst_ref, dst_row, sem):
    def tile(ref, row):
        start = row * SUB if isinstance(row, int) else pl.multiple_of(row * SUB, SUB)
        return ref.at[pl.ds(start, SUB), :]

    return pltpu.make_async_copy(tile(src_ref, src_row), tile(dst_ref, dst_row), sem)


def _expert_kernel(be_ref, bfirst_ref, slot_cur_ref, slot_nxt_ref, h2_ref, wg_ref, wu_ref, wd_ref,
                   y2_ref, xbuf_ref, ybuf_ref, wgb_ref, wub_ref, wdb_ref, gsem, ssem,
                   *, n_blocks, dump_base):
    del be_ref
    i = pl.program_id(0)
    row_mask = (1 << ROW_BITS) - 1

    def start_gather(slot_ref, buf):
        for r in range(BM):
            _tile_copy(h2_ref, slot_ref[0, r] & row_mask, xbuf_ref.at[buf], r, gsem.at[buf]).start()

    def wait_gather(buf):
        for r in range(BM):
            _tile_copy(h2_ref, 0, xbuf_ref.at[buf], r, gsem.at[buf]).wait()

    def start_scatter(buf):
        for r in range(BM):
            row = lax.shift_right_logical(slot_cur_ref[0, r], ROW_BITS)
            _tile_copy(ybuf_ref.at[buf], r, y2_ref, row, ssem).start()

    def wait_scatter(buf):
        for r in range(BM):
            _tile_copy(ybuf_ref.at[buf], r, y2_ref, 0, ssem).wait()

    @pl.when(i == 0)
    def _():
        start_gather(slot_cur_ref, 0)
        ybuf_ref[1] = jnp.zeros((BM * SUB, LANES), F32)
        for r in range(BM):
            _tile_copy(ybuf_ref.at[1], r, y2_ref, dump_base + r, ssem).start()

    @pl.when(bfirst_ref[i] == 1)
    def _():
        wgb_ref[...] = wg_ref[...].astype(BF16)
        wub_ref[...] = wu_ref[...].astype(BF16)
        wdb_ref[...] = wd_ref[...].astype(BF16)

    def step(buf):
        start_gather(slot_nxt_ref, 1 - buf)
        wait_gather(buf)
        x = _load_row_tiles(xbuf_ref.at[buf], BM).astype(BF16)
        g = jnp.dot(x, wgb_ref[...], preferred_element_type=F32)
        u = jnp.dot(x, wub_ref[...], preferred_element_type=F32)
        a = (jax.nn.silu(g) * u).astype(BF16)
        y = jnp.dot(a, wdb_ref[...], preferred_element_type=F32)
        wait_scatter(1 - buf)
        _store_row_tiles(ybuf_ref.at[buf], y)
        start_scatter(buf)

        @pl.when(i == n_blocks - 1)
        def _():
            wait_gather(1 - buf)
            wait_scatter(buf)

    for buf in range(2):
        pl.when(i % 2 == buf)(functools.partial(step, buf))


def _expert_call(layer, blk_e, blk_first, slots, h2_all, w_gate, w_up, w_down, n_y_rows,
                 dump_base):
    n_blocks = blk_e.shape[0]
    wmap = lambda i, be, bf: (layer, be[i], 0, 0)
    grid_spec = pltpu.PrefetchScalarGridSpec(
        num_scalar_prefetch=2,
        grid=(n_blocks,),
        in_specs=[
            pl.BlockSpec((None, 1, BM), lambda i, be, bf: (i, 0, 0), memory_space=pltpu.SMEM),
            pl.BlockSpec((None, 1, BM), lambda i, be, bf: (jnp.minimum(i + 1, n_blocks - 1), 0, 0),
                         memory_space=pltpu.SMEM),
            _ANY,
            pl.BlockSpec((None, None, D_MODEL, D_EXPERT), wmap),
            pl.BlockSpec((None, None, D_MODEL, D_EXPERT), wmap),
            pl.BlockSpec((None, None, D_EXPERT, D_MODEL), wmap),
        ],
        out_specs=_ANY,
        scratch_shapes=[
            pltpu.VMEM((2, BM * SUB, LANES), F32),
            pltpu.VMEM((2, BM * SUB, LANES), F32),
            pltpu.VMEM((D_MODEL, D_EXPERT), BF16),
            pltpu.VMEM((D_MODEL, D_EXPERT), BF16),
            pltpu.VMEM((D_EXPERT, D_MODEL), BF16),
            pltpu.SemaphoreType.DMA((2,)),
            pltpu.SemaphoreType.DMA,
        ],
    )
    return pl.pallas_call(
        functools.partial(_expert_kernel, n_blocks=n_blocks, dump_base=dump_base),
        grid_spec=grid_spec,
        out_shape=jax.ShapeDtypeStruct((n_y_rows * SUB, LANES), F32),
        name="expert_mlp",
        compiler_params=pltpu.CompilerParams(
            dimension_semantics=("arbitrary",),
            vmem_limit_bytes=VMEM_LIMIT),
    )(blk_e, blk_first, slots, slots, h2_all, w_gate, w_up, w_down)


def _combine_kernel(x1_ref, route_ref, y0_ref, y1_ref, g2p_ref, g2s_ref, ln2g_ref, ln2b_ref,
                    out_ref, *, tm, n_prompt_tiles):
    i = pl.program_id(0)
    route = route_ref[...]
    f = route[:, 2:3] * _load_row_tiles(y0_ref, tm) + route[:, 3:4] * _load_row_tiles(y1_ref, tm)
    g2 = jnp.where(i < n_prompt_tiles, g2p_ref[...], g2s_ref[...])
    r = DEEPNORM_ALPHA * x1_ref[...] + (1.0 + g2) * f
    out_ref[...] = _layer_norm(r, ln2g_ref[...], ln2b_ref[...])


def _combine_call(x1_all, route_all, y2, g2_p, g2_s, ln2_g, ln2_b, n_seq, seq_len):
    tm = TM_ROW
    t_all = x1_all.shape[0]
    n_tiles = t_all // tm
    tiles_per_seq = seq_len // tm
    return pl.pallas_call(
        functools.partial(_combine_kernel, tm=tm, n_prompt_tiles=n_seq * tiles_per_seq),
        grid=(n_tiles,),
        in_specs=[
            pl.BlockSpec((tm, D_MODEL), lambda i: (i, 0)),
            pl.BlockSpec((tm, LANES), lambda i: (i, 0)),
            pl.BlockSpec((tm * SUB, LANES), lambda i: (i, 0)),
            pl.BlockSpec((tm * SUB, LANES), lambda i: (n_tiles + i, 0)),
            pl.BlockSpec((None, 1, D_MODEL),
                         lambda i: (jnp.minimum(i // tiles_per_seq, n_seq - 1), 0, 0)),
            _full((tm, D_MODEL)),
            _full((1, D_MODEL)),
            _full((1, D_MODEL)),
        ],
        out_specs=pl.BlockSpec((tm, D_MODEL), lambda i: (i, 0)),
        out_shape=jax.ShapeDtypeStruct((t_all, D_MODEL), F32),
        name="combine_ln2",
        compiler_params=pltpu.CompilerParams(dimension_semantics=("arbitrary",)),
    )(x1_all, route_all, y2, y2, g2_p, g2_s, ln2_g, ln2_b)


def _split_bf16(w):
    hi = w.astype(BF16)
    lo = (w - hi.astype(F32)).astype(BF16)
    return jnp.concatenate([hi, lo], axis=1)


def _strict_lower(n):
    r = lax.broadcasted_iota(jnp.int32, (n, n), 0)
    c = lax.broadcasted_iota(jnp.int32, (n, n), 1)
    return (c < r).astype(BF16)


def kernel(x_prompt, x_sample, state_pool, c_prompt, c_sample, w_ada, b_ada, w_in, pool_w, pool_scale, sgu_norm_g, sgu_w, sgu_b, w_out, ln1_g, ln1_b, router_g_w, router_g_b, router_e_w, router_e_b, exp_w_gate, exp_w_up, exp_w_down, ln2_g, ln2_b):
    n_seq, seq_len, _ = x_prompt.shape
    n_dec = x_sample.shape[0]
    t_p = n_seq * seq_len
    t_all = t_p + n_dec
    assert n_dec == TM_ROW and t_p % n_dec == 0 and seq_len % TM_MIX == 0
    assert t_all % MAP_STEPS == 0
    n_assign = 2 * t_all
    n_blocks = -(-n_assign // BM) + N_EXPERTS
    n_slots = n_blocks * BM
    dump_base = n_assign
    n_y_rows = n_assign + BM
    assert n_y_rows <= (1 << (32 - ROW_BITS)) and t_all <= (1 << ROW_BITS)

    x_all = jnp.concatenate([x_prompt.reshape(t_p, D_MODEL), x_sample.reshape(n_dec, D_MODEL)])
    mod = _mod_call(jnp.concatenate([c_prompt, c_sample], axis=0), w_ada, b_ada)
    ltri_p = _strict_lower(TM_MIX)
    ltri_s = _strict_lower(n_dec)
    lane_pad = jnp.zeros((D_MODEL, LANES - N_EXPERTS - N_GROUPS), F32)
    expert_ids = jnp.arange(N_EXPERTS, dtype=jnp.int32)
    block_start = jnp.arange(n_blocks, dtype=jnp.int32) * BM
    fill_words = (dump_base + (jnp.arange(n_slots, dtype=jnp.int32) & (BM - 1))) << ROW_BITS

    h2_all = jnp.zeros((t_all * SUB, LANES), F32)
    route_all = jnp.zeros((t_all, LANES), F32)

    pool_p, pool_s, v_s = [], [], []
    for l in range(DEPTH):
        lw = {
            "w_in": w_in[l].astype(BF16),
            "w_out": w_out[l].astype(BF16),
            "pool_w": pool_w[l].astype(BF16),
            "pool_scale": pool_scale[l].reshape(1, D_POOL),
            "sgu_g": sgu_norm_g[l].reshape(1, D_SGU),
            "sgu_w": sgu_w[l],
            "sgu_b_t": sgu_b[l].T,
            "sgu_w0": jnp.repeat(sgu_w[l, :, 0, 0], SGU_HEAD_DIM).reshape(1, D_SGU),
            "sgu_b0": jnp.repeat(sgu_b[l, :, 0], SGU_HEAD_DIM).reshape(1, D_SGU),
            "ln1_g": ln1_g[l].reshape(1, D_MODEL),
            "ln1_b": ln1_b[l].reshape(1, D_MODEL),
            "w_r": _split_bf16(jnp.concatenate([router_e_w[l], router_g_w[l], lane_pad], axis=1)),
            "b_r": jnp.concatenate([router_e_b[l], router_g_b[l],
                                    jnp.zeros((LANES - N_EXPERTS - N_GROUPS,), F32)]).reshape(1, LANES),
            "ltri_p": ltri_p,
            "ltri_s": ltri_s,
        }
        mod_p = mod[l, :n_seq].reshape(n_seq, 6, D_MODEL)
        mod_s = mod[l, n_seq:]

        x1_all, h2_all, route_all, nbuf_p, cnt_p = _mix_prompt_call(
            x_all, h2_all, route_all, mod_p, lw, n_seq, seq_len)
        hist = jnp.transpose(state_pool[l], (1, 0, 2))
        x1_all, h2_all, route_all, nbuf_s, vn_s, cnt = _mix_sample_call(
            x1_all, h2_all, route_all, mod_s, hist, lw, cnt_p, n_dec, t_p)
        pool_p.append(nbuf_p[:, HIST - POOL_BUF:])
        pool_s.append(jnp.transpose(nbuf_s, (1, 0, 2)))
        v_s.append(vn_s.reshape(n_dec, 1, D_SGU))

        counts = cnt[0, :N_EXPERTS].astype(jnp.int32)
        padded = ((counts + BM - 1) // BM) * BM
        pends = jnp.cumsum(padded)
        pstarts = pends - padded
        e_sel = route_all[:, 0:2].astype(jnp.int32)
        rank = route_all[:, 4:6].astype(jnp.int32)
        dest = rank + jnp.sum(jnp.where(e_sel[..., None] == expert_ids, pstarts, 0), axis=-1)
        slots = _slot_map_call(dest.T.reshape(n_assign), fill_words)
        slots = slots.reshape(n_blocks, 1, BM)

        blk_valid = block_start < pends[-1]
        blk_e = jnp.sum((pends[None, :] <= block_start[:, None]).astype(jnp.int32), axis=1)
        blk_e = jnp.minimum(blk_e, N_EXPERTS - 1)
        last_e = jnp.max(jnp.where(blk_valid, blk_e, 0))
        blk_e = jnp.where(blk_valid, blk_e, last_e).astype(jnp.int32)
        blk_first = (blk_valid & jnp.any(pstarts[None, :] == block_start[:, None], axis=1)
                     ).astype(jnp.int32)

        y2 = _expert_call(l, blk_e, blk_first, slots, h2_all, exp_w_gate, exp_w_up, exp_w_down,
                          n_y_rows, dump_base)
        x_all = _combine_call(x1_all, route_all, y2, mod_p[:, 5:6, :], mod_s[:, 5 * D_MODEL:],
                              ln2_g[l].reshape(1, D_MODEL), ln2_b[l].reshape(1, D_MODEL),
                              n_seq, seq_len)

    return (x_all[:t_p].reshape(n_seq, seq_len, D_MODEL),
            x_all[t_p:].reshape(n_dec, 1, D_MODEL),
            jnp.stack(pool_p, axis=0),
            jnp.stack(pool_s, axis=0),
            jnp.stack(v_s, axis=0))
```

```python
import functools

import jax
import jax.numpy as jnp
from jax import lax
from jax.experimental import pallas as pl
from jax.experimental.pallas import tpu as pltpu

D_MODEL = 1024
DEPTH = 4
PAST_LEN = 16384
D_POOL = 512
N_POOL_GROUPS = 4
POOL_GROUP_DIM = 128
POOL_WINDOWS = (2, 4, 8, 16)
POOL_BUF = 15
HIST = 16
D_SGU = 512
SGU_HEADS = 4
SGU_HEAD_DIM = 128
CHUNK = 128
N_GROUPS = 4
EXPERTS_PER_GROUP = 8
N_EXPERTS = 32
D_EXPERT = 512
DEEPNORM_ALPHA = (2.0 * DEPTH) ** 0.25
LN_EPS = 1e-5

LANES = 128
SUB = 8
assert D_MODEL == SUB * LANES
ROUTE_G_LANE = N_EXPERTS
SECOND_LANE = 64

TM_MIX = 512
TM_ROW = 256
BM = 256
MAP_STEPS = 8
ROW_BITS = 16

F32 = jnp.float32
BF16 = jnp.bfloat16
VMEM_LIMIT = 48 * 1024 * 1024


def _layer_norm(r, g, b):
    mu = jnp.mean(r, axis=-1, keepdims=True)
    rc = r - mu
    var = jnp.mean(rc * rc, axis=-1, keepdims=True)
    return rc * lax.rsqrt(var + LN_EPS) * g + b


def _head_norm(vh, g):
    mu = jnp.mean(vh, axis=-1, keepdims=True)
    vc = vh - mu
    var = jnp.mean(vc * vc, axis=-1, keepdims=True)
    return vc * lax.rsqrt(var + LN_EPS) * g


def _roll_half(row):
    return pltpu.roll(jnp.broadcast_to(row, (8, LANES)), SECOND_LANE, 1)[0:1]


def _store_row_tiles(ref, val):
    rows = val.shape[0]
    for c in range(SUB):
        ref[pl.ds(c, rows, stride=SUB), :] = val[:, c * LANES:(c + 1) * LANES]


def _load_row_tiles(ref, rows):
    return jnp.concatenate(
        [ref[pl.ds(c, rows, stride=SUB), :] for c in range(SUB)], axis=1)


def _route(logits, ltri_ref, run_ref):
    tm = logits.shape[0]
    lane = lax.broadcasted_iota(jnp.int32, (tm, LANES), 1)
    lanef = lane.astype(F32)
    neg = -jnp.inf
    is_g = (lane >= ROUTE_G_LANE) & (lane < ROUTE_G_LANE + N_GROUPS)
    glm = jnp.where(is_g, logits, neg)
    gmax = jnp.max(glm, axis=1, keepdims=True)
    g_idx = jnp.min(jnp.where(glm == gmax, lanef - ROUTE_G_LANE, 1e4), axis=1, keepdims=True)
    p_g = 1.0 / jnp.sum(jnp.exp(glm - gmax), axis=1, keepdims=True)

    in_grp = (lane < N_EXPERTS) & ((lane >> 3).astype(F32) == g_idx)
    elm = jnp.where(in_grp, logits, neg)
    m1 = jnp.max(elm, axis=1, keepdims=True)
    i1 = jnp.min(jnp.where(elm == m1, lanef, 1e4), axis=1, keepdims=True)
    elm2 = jnp.where(lanef == i1, neg, elm)
    m2 = jnp.max(elm2, axis=1, keepdims=True)
    i2 = jnp.min(jnp.where(elm2 == m2, lanef, 1e4), axis=1, keepdims=True)
    e21 = jnp.exp(m2 - m1)
    den = 1.0 + e21
    w0 = (1.0 / den) * p_g
    w1 = (e21 / den) * p_g

    hit0 = lanef == i1
    hit1 = lanef == i2 + SECOND_LANE
    oh = jnp.where(hit0 | hit1, 1.0, 0.0)
    before = jnp.dot(ltri_ref[...], oh.astype(BF16), preferred_element_type=F32)
    tot = jnp.sum(oh, axis=0, keepdims=True)
    lane1 = lax.broadcasted_iota(jnp.int32, (1, LANES), 1)
    tot0 = jnp.where(lane1 < N_EXPERTS, tot, 0.0)
    tot1 = jnp.where(lane1 >= SECOND_LANE, tot, 0.0)
    run = run_ref[...]
    base = run + _roll_half(run + tot0)
    val = before + base
    rank0 = jnp.sum(jnp.where(hit0, val, 0.0), axis=1, keepdims=True)
    rank1 = jnp.sum(jnp.where(hit1, val, 0.0), axis=1, keepdims=True)
    run_ref[...] = run + tot0 + _roll_half(tot1)

    out = jnp.where(lane == 0, i1, 0.0)
    out = jnp.where(lane == 1, i2, out)
    out = jnp.where(lane == 2, w0, out)
    out = jnp.where(lane == 3, w1, out)
    out = jnp.where(lane == 4, rank0, out)
    out = jnp.where(lane == 5, rank1, out)
    return out


def _finish_mix(x, mix, g1, sh2, sc2, ln1g_ref, ln1b_ref, wr_ref, br_ref, ltri_ref, run_ref,
                x1_ref, h2_ref, route_ref):
    r = DEEPNORM_ALPHA * x + (1.0 + g1) * mix
    x1 = _layer_norm(r, ln1g_ref[...], ln1b_ref[...])
    x1_ref[...] = x1
    h2 = x1 * (1.0 + sc2) + sh2
    _store_row_tiles(h2_ref, h2)
    hi = h2.astype(BF16)
    lo = (h2 - hi.astype(F32)).astype(BF16)
    both = jnp.dot(hi, wr_ref[...], preferred_element_type=F32)
    logits = (both[:, :LANES] + both[:, LANES:]
              + jnp.dot(lo, wr_ref[:, :LANES], preferred_element_type=F32) + br_ref[...])
    route_ref[...] = _route(logits, ltri_ref, run_ref)


def _masked_sgu_w(sguw_ref, hd):
    t = lax.broadcasted_iota(jnp.int32, (CHUNK, CHUNK), 0)
    s = lax.broadcasted_iota(jnp.int32, (CHUNK, CHUNK), 1)
    return jnp.where(s <= t, sguw_ref[hd], 0.0).astype(BF16)


def _full(shape):
    nd = len(shape)
    return pl.BlockSpec(shape, lambda *_: (0,) * nd)


_ANY = pl.BlockSpec(memory_space=pl.ANY)


def _mod_kernel(c_ref, w_ref, b_ref, o_ref):
    s = jax.nn.silu(c_ref[...])
    o_ref[...] = jnp.dot(s.astype(BF16), w_ref[...].astype(BF16),
                         preferred_element_type=F32) + b_ref[...]


def _mod_call(c_all, w_ada, b_ada):
    n = c_all.shape[0]
    tn = 1024
    return pl.pallas_call(
        _mod_kernel,
        grid=(DEPTH, 6 * D_MODEL // tn),
        in_specs=[
            pl.BlockSpec((n, D_MODEL), lambda l, j: (0, 0)),
            pl.BlockSpec((None, D_MODEL, tn), lambda l, j: (l, 0, j)),
            pl.BlockSpec((None, 1, tn), lambda l, j: (l, 0, j)),
        ],
        out_specs=pl.BlockSpec((None, n, tn), lambda l, j: (l, 0, j)),
        out_shape=jax.ShapeDtypeStruct((DEPTH, n, 6 * D_MODEL), F32),
        name="adaln_mod",
        compiler_params=pltpu.CompilerParams(
            dimension_semantics=("arbitrary", "arbitrary")),
    )(c_all, w_ada, b_ada.reshape(DEPTH, 1, 6 * D_MODEL))


def _mix_prompt_kernel(x_ref, mod_ref, win_ref, wout_ref, poolw_ref, pscale_ref, sgug_ref,
                       sguw_ref, sgub_ref, ln1g_ref, ln1b_ref, wr_ref, br_ref, ltri_ref,
                       h2_in_ref, route_in_ref,
                       x1_ref, h2_ref, route_ref, nbuf_ref, cnt_ref,
                       pe_ref, mixin_ref, run_ref, *, tm, tiles_per_seq):
    del h2_in_ref, route_in_ref
    b = pl.program_id(0)
    j = pl.program_id(1)

    @pl.when((b == 0) & (j == 0))
    def _():
        run_ref[...] = jnp.zeros_like(run_ref)

    @pl.when(j == 0)
    def _():
        pe_ref[0:HIST, :] = jnp.zeros((HIST, D_POOL), F32)

    x = x_ref[...]
    mod = mod_ref[...]
    sh1, sc1, g1, sh2, sc2 = (mod[i:i + 1] for i in range(5))
    h = x * (1.0 + sc1) + sh1
    proj = jnp.dot(h.astype(BF16), win_ref[...], preferred_element_type=F32)
    p = proj[:, :D_POOL]
    u = proj[:, D_POOL:D_POOL + D_SGU]
    v = proj[:, D_POOL + D_SGU:]

    pe_ref[HIST:HIST + tm, :] = p
    pos = j * tm + lax.broadcasted_iota(jnp.int32, (tm, 1), 0)
    for g, w in enumerate(POOL_WINDOWS):
        lo, hi = g * POOL_GROUP_DIM, (g + 1) * POOL_GROUP_DIM
        pg = p[:, lo:hi]
        s = pg
        for k in range(1, w):
            s = s + pe_ref[pl.ds(HIST - k, tm), lo:hi]
        cnt = jnp.minimum(pos + 1, w).astype(F32)
        d = s / cnt - pg
        a = jnp.dot(d.astype(BF16), poolw_ref[g], preferred_element_type=F32)
        mixin_ref[:, lo:hi] = (a * pscale_ref[:, lo:hi]).astype(BF16)

    @pl.when(j == tiles_per_seq - 1)
    def _():
        nbuf_ref[...] = pe_ref[tm:tm + HIST, :]

    pe_ref[0:HIST, :] = pe_ref[tm:tm + HIST, :]

    for hd in range(SGU_HEADS):
        lo, hi = hd * SGU_HEAD_DIM, (hd + 1) * SGU_HEAD_DIM
        vn = _head_norm(v[:, lo:hi], sgug_ref[:, lo:hi]).astype(BF16)
        ws = _masked_sgu_w(sguw_ref, hd)
        bcol = sgub_ref[:, hd:hd + 1]
        for c in range(tm // CHUNK):
            r0, r1 = c * CHUNK, (c + 1) * CHUNK
            z = jnp.dot(ws, vn[r0:r1], preferred_element_type=F32) + bcol
            mixin_ref[r0:r1, D_POOL + lo:D_POOL + hi] = (u[r0:r1, lo:hi] * z).astype(BF16)

    mix = jnp.dot(mixin_ref[...], wout_ref[...], preferred_element_type=F32)
    _finish_mix(x, mix, g1, sh2, sc2, ln1g_ref, ln1b_ref, wr_ref, br_ref, ltri_ref, run_ref,
                x1_ref, h2_ref, route_ref)
    cnt_ref[...] = run_ref[...]


def _mix_prompt_call(x_all, h2_buf, route_buf, mod_p, lw, n_seq, seq_len):
    tm = TM_MIX
    tps = seq_len // tm
    t_all = x_all.shape[0]
    row = lambda b, j: (b * tps + j, 0)
    return pl.pallas_call(
        functools.partial(_mix_prompt_kernel, tm=tm, tiles_per_seq=tps),
        grid=(n_seq, tps),
        in_specs=[
            pl.BlockSpec((tm, D_MODEL), row),
            pl.BlockSpec((None, 6, D_MODEL), lambda b, j: (b, 0, 0)),
            _full((D_MODEL, D_POOL + 2 * D_SGU)),
            _full((D_POOL + D_SGU, D_MODEL)),
            _full((N_POOL_GROUPS, POOL_GROUP_DIM, POOL_GROUP_DIM)),
            _full((1, D_POOL)),
            _full((1, D_SGU)),
            _full((SGU_HEADS, CHUNK, CHUNK)),
            _full((CHUNK, SGU_HEADS)),
            _full((1, D_MODEL)),
            _full((1, D_MODEL)),
            _full((D_MODEL, 2 * LANES)),
            _full((1, LANES)),
            _full((tm, tm)),
            _ANY, _ANY,
        ],
        out_specs=[
            pl.BlockSpec((tm, D_MODEL), row),
            pl.BlockSpec((tm * SUB, LANES), row),
            pl.BlockSpec((tm, LANES), row),
            pl.BlockSpec((None, HIST, D_POOL), lambda b, j: (b, 0, 0)),
            pl.BlockSpec((1, LANES), lambda b, j: (0, 0)),
        ],
        out_shape=[
            jax.ShapeDtypeStruct((t_all, D_MODEL), F32),
            jax.ShapeDtypeStruct((t_all * SUB, LANES), F32),
            jax.ShapeDtypeStruct((t_all, LANES), F32),
            jax.ShapeDtypeStruct((n_seq, HIST, D_POOL), F32),
            jax.ShapeDtypeStruct((1, LANES), F32),
        ],
        scratch_shapes=[
            pltpu.VMEM((HIST + tm, D_POOL), F32),
            pltpu.VMEM((tm, D_POOL + D_SGU), BF16),
            pltpu.VMEM((1, LANES), F32),
        ],
        input_output_aliases={0: 0, 14: 1, 15: 2},
        name="mix_prompt",
        compiler_params=pltpu.CompilerParams(
            dimension_semantics=("arbitrary", "arbitrary"),
            vmem_limit_bytes=VMEM_LIMIT),
    )(x_all, mod_p, lw["w_in"], lw["w_out"], lw["pool_w"], lw["pool_scale"], lw["sgu_g"],
      lw["sgu_w"], lw["sgu_b_t"], lw["ln1_g"], lw["ln1_b"], lw["w_r"], lw["b_r"], lw["ltri_p"],
      h2_buf, route_buf)


def _mix_sample_kernel(x_ref, mod_ref, hist_ref, win_ref, wout_ref, poolw_ref, pscale_ref,
                       sgug_ref, sguw0_ref, sgub0_ref, ln1g_ref, ln1b_ref, wr_ref, br_ref,
                       ltri_ref, cnt_in_ref, h2_in_ref, route_in_ref,
                       x1_ref, h2_ref, route_ref, nbuf_ref, vn_ref, cnt_ref,
                       mixin_ref, run_ref):
    del h2_in_ref, route_in_ref
    run_ref[...] = cnt_in_ref[...]
    x = x_ref[...]
    sh1, sc1, g1, sh2, sc2 = (mod_ref[:, i * D_MODEL:(i + 1) * D_MODEL] for i in range(5))
    h = x * (1.0 + sc1) + sh1
    proj = jnp.dot(h.astype(BF16), win_ref[...], preferred_element_type=F32)
    p = proj[:, :D_POOL]
    u = proj[:, D_POOL:D_POOL + D_SGU]
    v = proj[:, D_POOL + D_SGU:]

    for g, w in enumerate(POOL_WINDOWS):
        lo, hi = g * POOL_GROUP_DIM, (g + 1) * POOL_GROUP_DIM
        pg = p[:, lo:hi]
        s = pg
        for k in range(1, w):
            s = s + hist_ref[POOL_BUF - k, :, lo:hi]
        cnt = float(min(PAST_LEN + 1, w))
        d = s / cnt - pg
        a = jnp.dot(d.astype(BF16), poolw_ref[g], preferred_element_type=F32)
        mixin_ref[:, lo:hi] = (a * pscale_ref[:, lo:hi]).astype(BF16)

    for k in range(POOL_BUF - 1):
        nbuf_ref[k] = hist_ref[k + 1]
    nbuf_ref[POOL_BUF - 1] = p

    for hd in range(SGU_HEADS):
        lo, hi = hd * SGU_HEAD_DIM, (hd + 1) * SGU_HEAD_DIM
        vn = _head_norm(v[:, lo:hi], sgug_ref[:, lo:hi])
        vn_ref[:, lo:hi] = vn
        z = sguw0_ref[:, lo:hi].astype(BF16).astype(F32) * vn.astype(BF16).astype(F32) \
            + sgub0_ref[:, lo:hi]
        mixin_ref[:, D_POOL + lo:D_POOL + hi] = (u[:, lo:hi] * z).astype(BF16)

    mix = jnp.dot(mixin_ref[...], wout_ref[...], preferred_element_type=F32)
    _finish_mix(x, mix, g1, sh2, sc2, ln1g_ref, ln1b_ref, wr_ref, br_ref, ltri_ref, run_ref,
                x1_ref, h2_ref, route_ref)
    cnt_ref[...] = run_ref[...]


def _mix_sample_call(x1_all, h2_all, route_all, mod_s, hist, lw, cnt_in, n_dec, row0):
    blk = row0 // n_dec
    rows = lambda r, w: pl.BlockSpec((r, w), lambda i: (blk, 0))
    return pl.pallas_call(
        _mix_sample_kernel,
        grid=(1,),
        in_specs=[
            rows(n_dec, D_MODEL),
            _full((n_dec, 6 * D_MODEL)),
            _full((POOL_BUF, n_dec, D_POOL)),
            _full((D_MODEL, D_POOL + 2 * D_SGU)),
            _full((D_POOL + D_SGU, D_MODEL)),
            _full((N_POOL_GROUPS, POOL_GROUP_DIM, POOL_GROUP_DIM)),
            _full((1, D_POOL)),
            _full((1, D_SGU)),
            _full((1, D_SGU)),
            _full((1, D_SGU)),
            _full((1, D_MODEL)),
            _full((1, D_MODEL)),
            _full((D_MODEL, 2 * LANES)),
            _full((1, LANES)),
            _full((n_dec, n_dec)),
            _full((1, LANES)),
            _ANY, _ANY,
        ],
        out_specs=[
            rows(n_dec, D_MODEL),
            rows(n_dec * SUB, LANES),
            rows(n_dec, LANES),
            _full((POOL_BUF, n_dec, D_POOL)),
            _full((n_dec, D_SGU)),
            _full((1, LANES)),
        ],
        out_shape=[
            jax.ShapeDtypeStruct(x1_all.shape, F32),
            jax.ShapeDtypeStruct(h2_all.shape, F32),
            jax.ShapeDtypeStruct(route_all.shape, F32),
            jax.ShapeDtypeStruct((POOL_BUF, n_dec, D_POOL), F32),
            jax.ShapeDtypeStruct((n_dec, D_SGU), F32),
            jax.ShapeDtypeStruct((1, LANES), F32),
        ],
        scratch_shapes=[
            pltpu.VMEM((n_dec, D_POOL + D_SGU), BF16),
            pltpu.VMEM((1, LANES), F32),
        ],
        input_output_aliases={0: 0, 16: 1, 17: 2},
        name="mix_sample",
        compiler_params=pltpu.CompilerParams(
            dimension_semantics=("arbitrary",),
            vmem_limit_bytes=VMEM_LIMIT),
    )(x1_all, mod_s, hist, lw["w_in"], lw["w_out"], lw["pool_w"], lw["pool_scale"], lw["sgu_g"],
      lw["sgu_w0"], lw["sgu_b0"], lw["ln1_g"], lw["ln1_b"], lw["w_r"], lw["b_r"], lw["ltri_s"],
      cnt_in, h2_all, route_all)


MAP_UNROLL = 24


def _slot_map_kernel(dest_ref, fill_ref, out_ref, sem, *, chunk, choice_stride):
    i = pl.program_id(0)

    @pl.when(i == 0)
    def _():
        fill = pltpu.make_async_copy(fill_ref, out_ref, sem)
        fill.start()
        fill.wait()

    word_step = 1 + (1 << ROW_BITS)
    tok0 = (i % MAP_STEPS) * chunk
    word0 = tok0 + ((tok0 + (i // MAP_STEPS) * choice_stride) << ROW_BITS)

    def put(jj, word):
        for k in range(MAP_UNROLL):
            out_ref[dest_ref[0, jj * MAP_UNROLL + k]] = word + k * word_step
        return word + MAP_UNROLL * word_step

    lax.fori_loop(0, chunk // MAP_UNROLL, put, word0)


def _slot_map_call(dest_cm, fill_words, choice_stride):
    chunk = dest_cm.shape[0] // (2 * MAP_STEPS)
    assert chunk % MAP_UNROLL == 0
    return pl.pallas_call(
        functools.partial(_slot_map_kernel, chunk=chunk, choice_stride=choice_stride),
        grid=(2 * MAP_STEPS,),
        in_specs=[pl.BlockSpec((None, 1, chunk), lambda i: (i, 0, 0),
                               memory_space=pltpu.SMEM),
                  _ANY],
        out_specs=pl.BlockSpec(memory_space=pltpu.SMEM),
        out_shape=jax.ShapeDtypeStruct(fill_words.shape, jnp.int32),
        scratch_shapes=[pltpu.SemaphoreType.DMA],
        name="slot_map",
        compiler_params=pltpu.CompilerParams(dimension_semantics=("arbitrary",)),
    )(dest_cm.reshape(2 * MAP_STEPS, 1, chunk), fill_words)


def _tile_copy(src_ref, src_row, dst_ref, dst_row, sem):
    def tile(ref, row):
        start = row * SUB if isinstance(row, int) else pl.multiple_of(row * SUB, SUB)
        return ref.at[pl.ds(start, SUB), :]

    return pltpu.make_async_copy(tile(src_ref, src_row), tile(dst_ref, dst_row), sem)


def _expert_kernel(be_ref, bfirst_ref, nused_ref, slot_cur_ref, slot_nxt_ref, h2_ref,
                   wg_ref, wu_ref, wd_ref,
                   y2_ref, xbuf_ref, ybuf_ref, wgb_ref, wub_ref, wdb_ref, gsem, ssem,
                   *, dump_base, unowned_rows):
    del be_ref
    i = pl.program_id(0)
    n_used = nused_ref[0]
    row_mask = (1 << ROW_BITS) - 1

    def start_gather(slot_ref, buf):
        for r in range(BM):
            _tile_copy(h2_ref, slot_ref[0, r] & row_mask, xbuf_ref.at[buf], r,
                       gsem.at[buf]).start(priority=r % 2)

    def wait_gather(buf):
        for r in range(BM):
            _tile_copy(h2_ref, 0, xbuf_ref.at[buf], r, gsem.at[buf]).wait()

    def start_scatter(buf):
        for r in range(BM):
            row = lax.shift_right_logical(slot_cur_ref[0, r], ROW_BITS)
            _tile_copy(ybuf_ref.at[buf], r, y2_ref, row, ssem).start(priority=r % 2)

    def wait_scatter(buf):
        for r in range(BM):
            _tile_copy(ybuf_ref.at[buf], r, y2_ref, 0, ssem).wait()

    @pl.when(i == 0)
    def _():
        start_gather(slot_cur_ref, 0)
        ybuf_ref[1] = jnp.zeros((BM * SUB, LANES), F32)
        unowned = [row for lo, hi in unowned_rows for row in range(lo, hi)]
        assert len(unowned) <= BM
        for r, row in enumerate(unowned):
            _tile_copy(ybuf_ref.at[1], r, y2_ref, row, ssem).start(priority=r % 2)
        for r, row in enumerate(unowned):
            _tile_copy(ybuf_ref.at[1], r, y2_ref, row, ssem).wait()
        for r in range(BM):
            _tile_copy(ybuf_ref.at[1], r, y2_ref, dump_base + r, ssem).start(priority=r % 2)

    @pl.when(bfirst_ref[i] == 1)
    def _():
        wgb_ref[...] = wg_ref[...].astype(BF16)
        wub_ref[...] = wu_ref[...].astype(BF16)
        wdb_ref[...] = wd_ref[...].astype(BF16)

    def step(buf):
        start_gather(slot_nxt_ref, 1 - buf)
        wait_gather(buf)
        x = _load_row_tiles(xbuf_ref.at[buf], BM).astype(BF16)
        g = jnp.dot(x, wgb_ref[...], preferred_element_type=F32)
        u = jnp.dot(x, wub_ref[...], preferred_element_type=F32)
        a = (jax.nn.silu(g) * u).astype(BF16)
        y = jnp.dot(a, wdb_ref[...], preferred_element_type=F32)
        wait_scatter(1 - buf)
        _store_row_tiles(ybuf_ref.at[buf], y)
        start_scatter(buf)

        @pl.when(i == n_used - 1)
        def _():
            wait_gather(1 - buf)
            wait_scatter(buf)

    for buf in range(2):
        pl.when((i < n_used) & (i % 2 == buf))(functools.partial(step, buf))


def _expert_call(layer, blk_e, blk_first, n_used, slots, h2_all, w_gate, w_up, w_down,
                 choice_stride):
    n_blocks = blk_e.shape[0]
    t_all = h2_all.shape[0] // SUB
    dump_base = 2 * choice_stride
    n_y_rows = dump_base + BM
    unowned_rows = ((t_all, choice_stride), (choice_stride + t_all, 2 * choice_stride))
    wmap = lambda i, be, bf, nu: (layer, be[i], 0, 0)
    grid_spec = pltpu.PrefetchScalarGridSpec(
        num_scalar_prefetch=3,
        grid=(n_blocks,),
        in_specs=[
            pl.BlockSpec((None, 1, BM), lambda i, be, bf, nu: (jnp.minimum(i, nu[0] - 1), 0, 0),
                         memory_space=pltpu.SMEM),
            pl.BlockSpec((None, 1, BM), lambda i, be, bf, nu: (jnp.minimum(i + 1, nu[0] - 1), 0, 0),
                         memory_space=pltpu.SMEM),
            _ANY,
            pl.BlockSpec((None, None, D_MODEL, D_EXPERT), wmap),
            pl.BlockSpec((None, None, D_MODEL, D_EXPERT), wmap),
            pl.BlockSpec((None, None, D_EXPERT, D_MODEL), wmap),
        ],
        out_specs=_ANY,
        scratch_shapes=[
            pltpu.VMEM((2, BM * SUB, LANES), F32),
            pltpu.VMEM((2, BM * SUB, LANES), F32),
            pltpu.VMEM((D_MODEL, D_EXPERT), BF16),
            pltpu.VMEM((D_MODEL, D_EXPERT), BF16),
            pltpu.VMEM((D_EXPERT, D_MODEL), BF16),
            pltpu.SemaphoreType.DMA((2,)),
            pltpu.SemaphoreType.DMA,
        ],
    )
    return pl.pallas_call(
        functools.partial(_expert_kernel, dump_base=dump_base, unowned_rows=unowned_rows),
        grid_spec=grid_spec,
        out_shape=jax.ShapeDtypeStruct((n_y_rows * SUB, LANES), F32),
        name="expert_mlp",
        compiler_params=pltpu.CompilerParams(
            dimension_semantics=("arbitrary",),
            vmem_limit_bytes=VMEM_LIMIT),
    )(blk_e, blk_first, n_used, slots, slots, h2_all, w_gate, w_up, w_down)


def _combine_kernel(x1_ref, route_ref, y0_ref, y1_ref, g2_ref, ln2g_ref, ln2b_ref, out_ref, *, tm):
    route = route_ref[...]
    f = route[:, 2:3] * _load_row_tiles(y0_ref, tm) + route[:, 3:4] * _load_row_tiles(y1_ref, tm)
    r = DEEPNORM_ALPHA * x1_ref[...] + (1.0 + g2_ref[...]) * f
    out_ref[...] = _layer_norm(r, ln2g_ref[...], ln2b_ref[...])


def _combine_call(x1_all, route_all, y2, g2, g2_spec, ln2_g, ln2_b, tm, row0, n_rows,
                  choice_stride, in_place):
    blk0 = row0 // tm
    blk1 = (choice_stride + row0) // tm
    assert row0 % tm == 0 and n_rows % tm == 0 and choice_stride % tm == 0
    rows = lambda i: (blk0 + i, 0)
    if in_place:
        out_spec = pl.BlockSpec((tm, D_MODEL), rows)
        out_shape = jax.ShapeDtypeStruct(x1_all.shape, F32)
        aliases = {0: 0}
    else:
        out_spec = pl.BlockSpec((tm, D_MODEL), lambda i: (i, 0))
        out_shape = jax.ShapeDtypeStruct((n_rows, D_MODEL), F32)
        aliases = {}
    return pl.pallas_call(
        functools.partial(_combine_kernel, tm=tm),
        grid=(n_rows // tm,),
        in_specs=[
            pl.BlockSpec((tm, D_MODEL), rows),
            pl.BlockSpec((tm, LANES), rows),
            pl.BlockSpec((tm * SUB, LANES), rows),
            pl.BlockSpec((tm * SUB, LANES), lambda i: (blk1 + i, 0)),
            g2_spec,
            _full((1, D_MODEL)),
            _full((1, D_MODEL)),
        ],
        out_specs=out_spec,
        out_shape=out_shape,
        input_output_aliases=aliases,
        name="combine_ln2",
        compiler_params=pltpu.CompilerParams(
            dimension_semantics=("arbitrary",),
            vmem_limit_bytes=VMEM_LIMIT),
    )(x1_all, route_all, y2, y2, g2, ln2_g, ln2_b)


def _split_bf16(w):
    hi = w.astype(BF16)
    lo = (w - hi.astype(F32)).astype(BF16)
    return jnp.concatenate([hi, lo], axis=1)


def _strict_lower(n):
    r = lax.broadcasted_iota(jnp.int32, (n, n), 0)
    c = lax.broadcasted_iota(jnp.int32, (n, n), 1)
    return (c < r).astype(BF16)


def kernel(x_prompt, x_sample, state_pool, c_prompt, c_sample, w_ada, b_ada, w_in, pool_w, pool_scale, sgu_norm_g, sgu_w, sgu_b, w_out, ln1_g, ln1_b, router_g_w, router_g_b, router_e_w, router_e_b, exp_w_gate, exp_w_up, exp_w_down, ln2_g, ln2_b):
    n_seq, seq_len, _ = x_prompt.shape
    n_dec = x_sample.shape[0]
    t_p = n_seq * seq_len
    t_all = t_p + n_dec
    assert t_p % n_dec == 0 and seq_len % TM_MIX == 0 and seq_len % TM_ROW == 0
    assert t_all % MAP_STEPS == 0
    n_assign = 2 * t_all
    n_blocks = -(-n_assign // BM) + N_EXPERTS
    n_slots = n_blocks * BM
    choice_stride = -(-t_all // TM_ROW) * TM_ROW
    dump_base = 2 * choice_stride
    n_y_rows = dump_base + BM
    assert n_y_rows <= (1 << (32 - ROW_BITS)) and t_all <= (1 << ROW_BITS)

    x_all = jnp.concatenate([x_prompt.reshape(t_p, D_MODEL), x_sample.reshape(n_dec, D_MODEL)])
    mod = _mod_call(jnp.concatenate([c_prompt, c_sample], axis=0), w_ada, b_ada)
    ltri_p = _strict_lower(TM_MIX)
    ltri_s = _strict_lower(n_dec)
    lane_pad = jnp.zeros((D_MODEL, LANES - N_EXPERTS - N_GROUPS), F32)
    expert_ids = jnp.arange(N_EXPERTS, dtype=jnp.int32)
    block_start = jnp.arange(n_blocks, dtype=jnp.int32) * BM
    fill_words = (dump_base + (jnp.arange(n_slots, dtype=jnp.int32) & (BM - 1))) << ROW_BITS

    h2_all = jnp.zeros((t_all * SUB, LANES), F32)
    route_all = jnp.zeros((t_all, LANES), F32)

    pool_p, pool_s, v_s = [], [], []
    for l in range(DEPTH):
        lw = {
            "w_in": w_in[l].astype(BF16),
            "w_out": w_out[l].astype(BF16),
            "pool_w": pool_w[l].astype(BF16),
            "pool_scale": pool_scale[l].reshape(1, D_POOL),
            "sgu_g": sgu_norm_g[l].reshape(1, D_SGU),
            "sgu_w": sgu_w[l],
            "sgu_b_t": sgu_b[l].T,
            "sgu_w0": jnp.repeat(sgu_w[l, :, 0, 0], SGU_HEAD_DIM).reshape(1, D_SGU),
            "sgu_b0": jnp.repeat(sgu_b[l, :, 0], SGU_HEAD_DIM).reshape(1, D_SGU),
            "ln1_g": ln1_g[l].reshape(1, D_MODEL),
            "ln1_b": ln1_b[l].reshape(1, D_MODEL),
            "w_r": _split_bf16(jnp.concatenate([router_e_w[l], router_g_w[l], lane_pad], axis=1)),
            "b_r": jnp.concatenate([router_e_b[l], router_g_b[l],
                                    jnp.zeros((LANES - N_EXPERTS - N_GROUPS,), F32)]).reshape(1, LANES),
            "ltri_p": ltri_p,
            "ltri_s": ltri_s,
        }
        mod_p = mod[l, :n_seq].reshape(n_seq, 6, D_MODEL)
        mod_s = mod[l, n_seq:]

        x1_all, h2_all, route_all, nbuf_p, cnt_p = _mix_prompt_call(
            x_all, h2_all, route_all, mod_p, lw, n_seq, seq_len)
        hist = jnp.transpose(state_pool[l], (1, 0, 2))
        x1_all, h2_all, route_all, nbuf_s, vn_s, cnt = _mix_sample_call(
            x1_all, h2_all, route_all, mod_s, hist, lw, cnt_p, n_dec, t_p)
        pool_p.append(nbuf_p[:, HIST - POOL_BUF:])
        pool_s.append(jnp.transpose(nbuf_s, (1, 0, 2)))
        v_s.append(vn_s.reshape(n_dec, 1, D_SGU))

        counts = cnt[0, :N_EXPERTS].astype(jnp.int32)
        padded = ((counts + BM - 1) // BM) * BM
        pends = jnp.cumsum(padded)
        pstarts = pends - padded
        e_sel = route_all[:, 0:2].astype(jnp.int32)
        rank = route_all[:, 4:6].astype(jnp.int32)
        dest = rank + jnp.sum(jnp.where(e_sel[..., None] == expert_ids, pstarts, 0), axis=-1)
        slots = _slot_map_call(dest.T.reshape(n_assign), fill_words, choice_stride)
        slots = slots.reshape(n_blocks, 1, BM)

        blk_valid = block_start < pends[-1]
        n_used = (pends[-1:] // BM).astype(jnp.int32)
        blk_e = jnp.sum((pends[None, :] <= block_start[:, None]).astype(jnp.int32), axis=1)
        blk_e = jnp.minimum(blk_e, N_EXPERTS - 1)
        last_e = jnp.max(jnp.where(blk_valid, blk_e, 0))
        blk_e = jnp.where(blk_valid, blk_e, last_e).astype(jnp.int32)
        blk_first = (blk_valid & jnp.any(pstarts[None, :] == block_start[:, None], axis=1)
                     ).astype(jnp.int32)

        y2 = _expert_call(l, blk_e, blk_first, n_used, slots, h2_all,
                          exp_w_gate, exp_w_up, exp_w_down, choice_stride)

        ln2g = ln2_g[l].reshape(1, D_MODEL)
        ln2b = ln2_b[l].reshape(1, D_MODEL)
        tiles_per_seq = seq_len // TM_ROW
        g2_p_spec = pl.BlockSpec((None, 1, D_MODEL), lambda i: (i // tiles_per_seq, 0, 0))
        last = l == DEPTH - 1
        out_p = _combine_call(x1_all, route_all, y2, mod_p[:, 5:6, :], g2_p_spec, ln2g, ln2b,
                              TM_ROW, 0, t_p, choice_stride, in_place=not last)
        out_s = _combine_call(x1_all if last else out_p, route_all, y2, mod_s[:, 5 * D_MODEL:],
                              _full((n_dec, D_MODEL)), ln2g, ln2b,
                              n_dec, t_p, n_dec, choice_stride, in_place=not last)
        x_all = out_s

    return (out_p.reshape(n_seq, seq_len, D_MODEL),
            out_s.reshape(n_dec, 1, D_MODEL),
            jnp.stack(pool_p, axis=0),
            jnp.stack(pool_s, axis=0),
            jnp.stack(v_s, axis=0))
```

```python
import functools

import jax
import jax.numpy as jnp
from jax import lax
from jax.experimental import pallas as pl
from jax.experimental.pallas import tpu as pltpu

D_MODEL = 1024
DEPTH = 4
PAST_LEN = 16384
D_POOL = 512
N_POOL_GROUPS = 4
POOL_GROUP_DIM = 128
POOL_WINDOWS = (2, 4, 8, 16)
POOL_BUF = 15
HIST = 16
D_SGU = 512
SGU_HEADS = 4
SGU_HEAD_DIM = 128
CHUNK = 128
N_GROUPS = 4
EXPERTS_PER_GROUP = 8
N_EXPERTS = 32
D_EXPERT = 512
DEEPNORM_ALPHA = (2.0 * DEPTH) ** 0.25
LN_EPS = 1e-5

LANES = 128
SUB = 8
assert D_MODEL == SUB * LANES
ROUTE_G_LANE = N_EXPERTS
SECOND_LANE = 64

TM_MIX = 512
TM_ROW = 256
TM_DISPATCH = 384
BM = 256
MAP_STEPS = 8
ROW_BITS = 16

F32 = jnp.float32
BF16 = jnp.bfloat16
VMEM_LIMIT = 48 * 1024 * 1024


def _layer_norm(r, g, b):
    mu = jnp.mean(r, axis=-1, keepdims=True)
    rc = r - mu
    var = jnp.mean(rc * rc, axis=-1, keepdims=True)
    return rc * lax.rsqrt(var + LN_EPS) * g + b


def _head_norm(vh, g):
    mu = jnp.mean(vh, axis=-1, keepdims=True)
    vc = vh - mu
    var = jnp.mean(vc * vc, axis=-1, keepdims=True)
    return vc * lax.rsqrt(var + LN_EPS) * g


def _roll_half(row):
    return pltpu.roll(jnp.broadcast_to(row, (8, LANES)), SECOND_LANE, 1)[0:1]


def _store_row_tiles(ref, val):
    rows = val.shape[0]
    for c in range(SUB):
        ref[pl.ds(c, rows, stride=SUB), :] = val[:, c * LANES:(c + 1) * LANES]


def _load_row_tiles(ref, rows):
    return jnp.concatenate(
        [ref[pl.ds(c, rows, stride=SUB), :] for c in range(SUB)], axis=1)


def _route(logits, ltri_ref, run_ref):
    tm = logits.shape[0]
    lane = lax.broadcasted_iota(jnp.int32, (tm, LANES), 1)
    lanef = lane.astype(F32)
    neg = -jnp.inf
    is_g = (lane >= ROUTE_G_LANE) & (lane < ROUTE_G_LANE + N_GROUPS)
    glm = jnp.where(is_g, logits, neg)
    gmax = jnp.max(glm, axis=1, keepdims=True)
    g_idx = jnp.min(jnp.where(glm == gmax, lanef - ROUTE_G_LANE, 1e4), axis=1, keepdims=True)
    p_g = 1.0 / jnp.sum(jnp.exp(glm - gmax), axis=1, keepdims=True)

    in_grp = (lane < N_EXPERTS) & ((lane >> 3).astype(F32) == g_idx)
    elm = jnp.where(in_grp, logits, neg)
    m1 = jnp.max(elm, axis=1, keepdims=True)
    i1 = jnp.min(jnp.where(elm == m1, lanef, 1e4), axis=1, keepdims=True)
    elm2 = jnp.where(lanef == i1, neg, elm)
    m2 = jnp.max(elm2, axis=1, keepdims=True)
    i2 = jnp.min(jnp.where(elm2 == m2, lanef, 1e4), axis=1, keepdims=True)
    e21 = jnp.exp(m2 - m1)
    den = 1.0 + e21
    w0 = (1.0 / den) * p_g
    w1 = (e21 / den) * p_g

    hit0 = lanef == i1
    hit1 = lanef == i2 + SECOND_LANE
    oh = jnp.where(hit0 | hit1, 1.0, 0.0)
    before = jnp.dot(ltri_ref[...], oh.astype(BF16), preferred_element_type=F32)
    tot = jnp.sum(oh, axis=0, keepdims=True)
    lane1 = lax.broadcasted_iota(jnp.int32, (1, LANES), 1)
    tot0 = jnp.where(lane1 < N_EXPERTS, tot, 0.0)
    tot1 = jnp.where(lane1 >= SECOND_LANE, tot, 0.0)
    run = run_ref[...]
    base = run + _roll_half(run + tot0)
    val = before + base
    rank0 = jnp.sum(jnp.where(hit0, val, 0.0), axis=1, keepdims=True)
    rank1 = jnp.sum(jnp.where(hit1, val, 0.0), axis=1, keepdims=True)
    run_ref[...] = run + tot0 + _roll_half(tot1)

    out = jnp.where(lane == 0, i1, 0.0)
    out = jnp.where(lane == 1, i2, out)
    out = jnp.where(lane == 2, w0, out)
    out = jnp.where(lane == 3, w1, out)
    out = jnp.where(lane == 4, rank0, out)
    out = jnp.where(lane == 5, rank1, out)
    return out


def _finish_mix(x, mix, g1, sh2, sc2, ln1g_ref, ln1b_ref, wr_ref, br_ref, ltri_ref, run_ref,
                x1_ref, h2_ref, route_ref):
    r = DEEPNORM_ALPHA * x + (1.0 + g1) * mix
    x1 = _layer_norm(r, ln1g_ref[...], ln1b_ref[...])
    x1_ref[...] = x1
    h2 = x1 * (1.0 + sc2) + sh2
    _store_row_tiles(h2_ref, h2)
    hi = h2.astype(BF16)
    lo = (h2 - hi.astype(F32)).astype(BF16)
    both = jnp.dot(hi, wr_ref[...], preferred_element_type=F32)
    logits = (both[:, :LANES] + both[:, LANES:]
              + jnp.dot(lo, wr_ref[:, :LANES], preferred_element_type=F32) + br_ref[...])
    route_ref[...] = _route(logits, ltri_ref, run_ref)


def _masked_sgu_w(sguw_ref, hd):
    t = lax.broadcasted_iota(jnp.int32, (CHUNK, CHUNK), 0)
    s = lax.broadcasted_iota(jnp.int32, (CHUNK, CHUNK), 1)
    return jnp.where(s <= t, sguw_ref[hd], 0.0).astype(BF16)


def _full(shape):
    nd = len(shape)
    return pl.BlockSpec(shape, lambda *_: (0,) * nd)


_ANY = pl.BlockSpec(memory_space=pl.ANY)


def _mod_kernel(c_ref, w_ref, b_ref, o_ref):
    s = jax.nn.silu(c_ref[...])
    o_ref[...] = jnp.dot(s.astype(BF16), w_ref[...].astype(BF16),
                         preferred_element_type=F32) + b_ref[...]


def _mod_call(c_all, w_ada, b_ada):
    n = c_all.shape[0]
    tn = 1024
    return pl.pallas_call(
        _mod_kernel,
        grid=(DEPTH, 6 * D_MODEL // tn),
        in_specs=[
            pl.BlockSpec((n, D_MODEL), lambda l, j: (0, 0)),
            pl.BlockSpec((None, D_MODEL, tn), lambda l, j: (l, 0, j)),
            pl.BlockSpec((None, 1, tn), lambda l, j: (l, 0, j)),
        ],
        out_specs=pl.BlockSpec((None, n, tn), lambda l, j: (l, 0, j)),
        out_shape=jax.ShapeDtypeStruct((DEPTH, n, 6 * D_MODEL), F32),
        name="adaln_mod",
        compiler_params=pltpu.CompilerParams(
            dimension_semantics=("arbitrary", "arbitrary")),
    )(c_all, w_ada, b_ada.reshape(DEPTH, 1, 6 * D_MODEL))


def _mix_prompt_kernel(x_ref, mod_ref, win_ref, wout_ref, poolw_ref, pscale_ref, sgug_ref,
                       sguw_ref, sgub_ref, ln1g_ref, ln1b_ref, wr_ref, br_ref, ltri_ref,
                       h2_in_ref, route_in_ref,
                       x1_ref, h2_ref, route_ref, nbuf_ref, cnt_ref,
                       pe_ref, mixin_ref, run_ref, *, tm, tiles_per_seq):
    del h2_in_ref, route_in_ref
    b = pl.program_id(0)
    j = pl.program_id(1)

    @pl.when((b == 0) & (j == 0))
    def _():
        run_ref[...] = jnp.zeros_like(run_ref)

    @pl.when(j == 0)
    def _():
        pe_ref[0:HIST, :] = jnp.zeros((HIST, D_POOL), F32)

    x = x_ref[...]
    mod = mod_ref[...]
    sh1, sc1, g1, sh2, sc2 = (mod[i:i + 1] for i in range(5))
    h = x * (1.0 + sc1) + sh1
    proj = jnp.dot(h.astype(BF16), win_ref[...], preferred_element_type=F32)
    p = proj[:, :D_POOL]
    u = proj[:, D_POOL:D_POOL + D_SGU]
    v = proj[:, D_POOL + D_SGU:]

    pe_ref[HIST:HIST + tm, :] = p
    pos = j * tm + lax.broadcasted_iota(jnp.int32, (tm, 1), 0)
    for g, w in enumerate(POOL_WINDOWS):
        lo, hi = g * POOL_GROUP_DIM, (g + 1) * POOL_GROUP_DIM
        pg = p[:, lo:hi]
        s = pg
        for k in range(1, w):
            s = s + pe_ref[pl.ds(HIST - k, tm), lo:hi]
        cnt = jnp.minimum(pos + 1, w).astype(F32)
        d = s / cnt - pg
        a = jnp.dot(d.astype(BF16), poolw_ref[g], preferred_element_type=F32)
        mixin_ref[:, lo:hi] = (a * pscale_ref[:, lo:hi]).astype(BF16)

    @pl.when(j == tiles_per_seq - 1)
    def _():
        nbuf_ref[...] = pe_ref[tm:tm + HIST, :]

    pe_ref[0:HIST, :] = pe_ref[tm:tm + HIST, :]

    for hd in range(SGU_HEADS):
        lo, hi = hd * SGU_HEAD_DIM, (hd + 1) * SGU_HEAD_DIM
        vn = _head_norm(v[:, lo:hi], sgug_ref[:, lo:hi]).astype(BF16)
        ws = _masked_sgu_w(sguw_ref, hd)
        bcol = sgub_ref[:, hd:hd + 1]
        for c in range(tm // CHUNK):
            r0, r1 = c * CHUNK, (c + 1) * CHUNK
            z = jnp.dot(ws, vn[r0:r1], preferred_element_type=F32) + bcol
            mixin_ref[r0:r1, D_POOL + lo:D_POOL + hi] = (u[r0:r1, lo:hi] * z).astype(BF16)

    mix = jnp.dot(mixin_ref[...], wout_ref[...], preferred_element_type=F32)
    _finish_mix(x, mix, g1, sh2, sc2, ln1g_ref, ln1b_ref, wr_ref, br_ref, ltri_ref, run_ref,
                x1_ref, h2_ref, route_ref)
    cnt_ref[...] = run_ref[...]


def _mix_prompt_call(x_all, h2_buf, route_buf, mod_p, lw, n_seq, seq_len):
    tm = TM_MIX
    tps = seq_len // tm
    t_all = x_all.shape[0]
    row = lambda b, j: (b * tps + j, 0)
    return pl.pallas_call(
        functools.partial(_mix_prompt_kernel, tm=tm, tiles_per_seq=tps),
        grid=(n_seq, tps),
        in_specs=[
            pl.BlockSpec((tm, D_MODEL), row),
            pl.BlockSpec((None, 6, D_MODEL), lambda b, j: (b, 0, 0)),
            _full((D_MODEL, D_POOL + 2 * D_SGU)),
            _full((D_POOL + D_SGU, D_MODEL)),
            _full((N_POOL_GROUPS, POOL_GROUP_DIM, POOL_GROUP_DIM)),
            _full((1, D_POOL)),
            _full((1, D_SGU)),
            _full((SGU_HEADS, CHUNK, CHUNK)),
            _full((CHUNK, SGU_HEADS)),
            _full((1, D_MODEL)),
            _full((1, D_MODEL)),
            _full((D_MODEL, 2 * LANES)),
            _full((1, LANES)),
            _full((tm, tm)),
            _ANY, _ANY,
        ],
        out_specs=[
            pl.BlockSpec((tm, D_MODEL), row),
            pl.BlockSpec((tm * SUB, LANES), row),
            pl.BlockSpec((tm, LANES), row),
            pl.BlockSpec((None, HIST, D_POOL), lambda b, j: (b, 0, 0)),
            pl.BlockSpec((1, LANES), lambda b, j: (0, 0)),
        ],
        out_shape=[
            jax.ShapeDtypeStruct((t_all, D_MODEL), F32),
            jax.ShapeDtypeStruct((t_all * SUB, LANES), F32),
            jax.ShapeDtypeStruct((t_all, LANES), F32),
            jax.ShapeDtypeStruct((n_seq, HIST, D_POOL), F32),
            jax.ShapeDtypeStruct((1, LANES), F32),
        ],
        scratch_shapes=[
            pltpu.VMEM((HIST + tm, D_POOL), F32),
            pltpu.VMEM((tm, D_POOL + D_SGU), BF16),
            pltpu.VMEM((1, LANES), F32),
        ],
        input_output_aliases={0: 0, 14: 1, 15: 2},
        name="mix_prompt",
        compiler_params=pltpu.CompilerParams(
            dimension_semantics=("arbitrary", "arbitrary"),
            vmem_limit_bytes=VMEM_LIMIT),
    )(x_all, mod_p, lw["w_in"], lw["w_out"], lw["pool_w"], lw["pool_scale"], lw["sgu_g"],
      lw["sgu_w"], lw["sgu_b_t"], lw["ln1_g"], lw["ln1_b"], lw["w_r"], lw["b_r"], lw["ltri_p"],
      h2_buf, route_buf)


def _mix_sample_kernel(x_ref, mod_ref, hist_ref, win_ref, wout_ref, poolw_ref, pscale_ref,
                       sgug_ref, sguw0_ref, sgub0_ref, ln1g_ref, ln1b_ref, wr_ref, br_ref,
                       ltri_ref, cnt_in_ref, h2_in_ref, route_in_ref,
                       x1_ref, h2_ref, route_ref, nbuf_ref, vn_ref, cnt_ref,
                       mixin_ref, run_ref):
    del h2_in_ref, route_in_ref
    run_ref[...] = cnt_in_ref[...]
    x = x_ref[...]
    sh1, sc1, g1, sh2, sc2 = (mod_ref[:, i * D_MODEL:(i + 1) * D_MODEL] for i in range(5))
    h = x * (1.0 + sc1) + sh1
    proj = jnp.dot(h.astype(BF16), win_ref[...], preferred_element_type=F32)
    p = proj[:, :D_POOL]
    u = proj[:, D_POOL:D_POOL + D_SGU]
    v = proj[:, D_POOL + D_SGU:]

    for g, w in enumerate(POOL_WINDOWS):
        lo, hi = g * POOL_GROUP_DIM, (g + 1) * POOL_GROUP_DIM
        pg = p[:, lo:hi]
        s = pg
        for k in range(1, w):
            s = s + hist_ref[POOL_BUF - k, :, lo:hi]
        cnt = float(min(PAST_LEN + 1, w))
        d = s / cnt - pg
        a = jnp.dot(d.astype(BF16), poolw_ref[g], preferred_element_type=F32)
        mixin_ref[:, lo:hi] = (a * pscale_ref[:, lo:hi]).astype(BF16)

    for k in range(POOL_BUF - 1):
        nbuf_ref[k] = hist_ref[k + 1]
    nbuf_ref[POOL_BUF - 1] = p

    for hd in range(SGU_HEADS):
        lo, hi = hd * SGU_HEAD_DIM, (hd + 1) * SGU_HEAD_DIM
        vn = _head_norm(v[:, lo:hi], sgug_ref[:, lo:hi])
        vn_ref[:, lo:hi] = vn
        z = sguw0_ref[:, lo:hi].astype(BF16).astype(F32) * vn.astype(BF16).astype(F32) \
            + sgub0_ref[:, lo:hi]
        mixin_ref[:, D_POOL + lo:D_POOL + hi] = (u[:, lo:hi] * z).astype(BF16)

    mix = jnp.dot(mixin_ref[...], wout_ref[...], preferred_element_type=F32)
    _finish_mix(x, mix, g1, sh2, sc2, ln1g_ref, ln1b_ref, wr_ref, br_ref, ltri_ref, run_ref,
                x1_ref, h2_ref, route_ref)
    cnt_ref[...] = run_ref[...]


def _mix_sample_call(x1_all, h2_all, route_all, mod_s, hist, lw, cnt_in, n_dec, row0):
    blk = row0 // n_dec
    rows = lambda r, w: pl.BlockSpec((r, w), lambda i: (blk, 0))
    return pl.pallas_call(
        _mix_sample_kernel,
        grid=(1,),
        in_specs=[
            rows(n_dec, D_MODEL),
            _full((n_dec, 6 * D_MODEL)),
            _full((POOL_BUF, n_dec, D_POOL)),
            _full((D_MODEL, D_POOL + 2 * D_SGU)),
            _full((D_POOL + D_SGU, D_MODEL)),
            _full((N_POOL_GROUPS, POOL_GROUP_DIM, POOL_GROUP_DIM)),
            _full((1, D_POOL)),
            _full((1, D_SGU)),
            _full((1, D_SGU)),
            _full((1, D_SGU)),
            _full((1, D_MODEL)),
            _full((1, D_MODEL)),
            _full((D_MODEL, 2 * LANES)),
            _full((1, LANES)),
            _full((n_dec, n_dec)),
            _full((1, LANES)),
            _ANY, _ANY,
        ],
        out_specs=[
            rows(n_dec, D_MODEL),
            rows(n_dec * SUB, LANES),
            rows(n_dec, LANES),
            _full((POOL_BUF, n_dec, D_POOL)),
            _full((n_dec, D_SGU)),
            _full((1, LANES)),
        ],
        out_shape=[
            jax.ShapeDtypeStruct(x1_all.shape, F32),
            jax.ShapeDtypeStruct(h2_all.shape, F32),
            jax.ShapeDtypeStruct(route_all.shape, F32),
            jax.ShapeDtypeStruct((POOL_BUF, n_dec, D_POOL), F32),
            jax.ShapeDtypeStruct((n_dec, D_SGU), F32),
            jax.ShapeDtypeStruct((1, LANES), F32),
        ],
        scratch_shapes=[
            pltpu.VMEM((n_dec, D_POOL + D_SGU), BF16),
            pltpu.VMEM((1, LANES), F32),
        ],
        input_output_aliases={0: 0, 16: 1, 17: 2},
        name="mix_sample",
        compiler_params=pltpu.CompilerParams(
            dimension_semantics=("arbitrary",),
            vmem_limit_bytes=VMEM_LIMIT),
    )(x1_all, mod_s, hist, lw["w_in"], lw["w_out"], lw["pool_w"], lw["pool_scale"], lw["sgu_g"],
      lw["sgu_w0"], lw["sgu_b0"], lw["ln1_g"], lw["ln1_b"], lw["w_r"], lw["b_r"], lw["ltri_s"],
      cnt_in, h2_all, route_all)


MAP_UNROLL = 24


def _slot_map_kernel(dest_ref, fill_ref, out_ref, sem, *, chunk, choice_stride):
    i = pl.program_id(0)

    @pl.when(i == 0)
    def _():
        fill = pltpu.make_async_copy(fill_ref, out_ref, sem)
        fill.start()
        fill.wait()

    word_step = 1 + (1 << ROW_BITS)
    tok0 = (i % MAP_STEPS) * chunk
    word0 = tok0 + ((tok0 + (i // MAP_STEPS) * choice_stride) << ROW_BITS)

    def put(jj, word):
        for k in range(MAP_UNROLL):
            out_ref[dest_ref[0, jj * MAP_UNROLL + k]] = word + k * word_step
        return word + MAP_UNROLL * word_step

    lax.fori_loop(0, chunk // MAP_UNROLL, put, word0)


def _slot_map_call(dest_cm, fill_words, choice_stride):
    chunk = dest_cm.shape[0] // (2 * MAP_STEPS)
    assert chunk % MAP_UNROLL == 0
    return pl.pallas_call(
        functools.partial(_slot_map_kernel, chunk=chunk, choice_stride=choice_stride),
        grid=(2 * MAP_STEPS,),
        in_specs=[pl.BlockSpec((None, 1, chunk), lambda i: (i, 0, 0),
                               memory_space=pltpu.SMEM),
                  _ANY],
        out_specs=pl.BlockSpec(memory_space=pltpu.SMEM),
        out_shape=jax.ShapeDtypeStruct(fill_words.shape, jnp.int32),
        scratch_shapes=[pltpu.SemaphoreType.DMA],
        name="slot_map",
        compiler_params=pltpu.CompilerParams(dimension_semantics=("arbitrary",)),
    )(dest_cm.reshape(2 * MAP_STEPS, 1, chunk), fill_words)


def _tile_copy(src_ref, src_row, dst_ref, dst_row, sem):
    def tile(ref, row):
        start = row * SUB if isinstance(row, int) else pl.multiple_of(row * SUB, SUB)
        return ref.at[pl.ds(start, SUB), :]

    return pltpu.make_async_copy(tile(src_ref, src_row), tile(dst_ref, dst_row), sem)


def _dispatch_kernel(pends_ref, dest_ref, h2_ref, xs_ref, zero_ref, sem, *, tm, n_blocks, min_used):
    @pl.when(pl.program_id(0) == 0)
    def _():
        zero_ref[...] = jnp.zeros_like(zero_ref)
        n_used = lax.shift_right_logical(pends_ref[N_EXPERTS - 1], BM.bit_length() - 1)

        def block_copy(b):
            first = pl.multiple_of(b * (BM * SUB), BM * SUB)
            return pltpu.make_async_copy(zero_ref, xs_ref.at[pl.ds(first, BM * SUB), :], sem)

        def last_block(e):
            return lax.shift_right_logical(pends_ref[e], BM.bit_length() - 1) - 1

        for act in ("start", "wait"):
            for e in range(N_EXPERTS):
                prev_end = pends_ref[e - 1] if e else 0
                pl.when(pends_ref[e] > prev_end)(
                    lambda e=e, act=act: getattr(block_copy(last_block(e)), act)())
            for b in range(min_used, n_blocks):
                pl.when(b >= n_used)(lambda b=b, act=act: getattr(block_copy(b), act)())

    for r in range(tm):
        for k in range(2):
            _tile_copy(h2_ref, r, xs_ref, dest_ref[0, 2 * r + k], sem).start(priority=k)
    for r in range(tm):
        for k in range(2):
            _tile_copy(h2_ref, r, xs_ref, 0, sem).wait()


def _dispatch_call(pends, dest, h2_all, n_blocks, min_used):
    tm = TM_DISPATCH
    n_tiles = dest.shape[0] // tm
    grid_spec = pltpu.PrefetchScalarGridSpec(
        num_scalar_prefetch=1,
        grid=(n_tiles,),
        in_specs=[
            pl.BlockSpec((None, 1, 2 * tm), lambda i, pe: (i, 0, 0), memory_space=pltpu.SMEM),
            pl.BlockSpec((tm * SUB, LANES), lambda i, pe: (i, 0)),
        ],
        out_specs=_ANY,
        scratch_shapes=[pltpu.VMEM((BM * SUB, LANES), F32), pltpu.SemaphoreType.DMA],
    )
    return pl.pallas_call(
        functools.partial(_dispatch_kernel, tm=tm, n_blocks=n_blocks, min_used=min_used),
        grid_spec=grid_spec,
        out_shape=jax.ShapeDtypeStruct((n_blocks * BM * SUB, LANES), F32),
        name="dispatch_rows",
        compiler_params=pltpu.CompilerParams(dimension_semantics=("arbitrary",)),
    )(pends, dest.reshape(n_tiles, 1, 2 * tm), h2_all)


def _expert_kernel(be_ref, bfirst_ref, nused_ref, slot_ref, xs_ref, wg_ref, wu_ref, wd_ref,
                   y2_ref, ybuf_ref, wgb_ref, wub_ref, wdb_ref, ssem,
                   *, dump_base, unowned_rows):
    del be_ref
    i = pl.program_id(0)
    n_used = nused_ref[0]

    def start_scatter(buf):
        for r in range(BM):
            row = lax.shift_right_logical(slot_ref[0, r], ROW_BITS)
            _tile_copy(ybuf_ref.at[buf], r, y2_ref, row, ssem).start(priority=r % 2)

    def wait_scatter(buf):
        for r in range(BM):
            _tile_copy(ybuf_ref.at[buf], r, y2_ref, 0, ssem).wait()

    @pl.when(i == 0)
    def _():
        ybuf_ref[1] = jnp.zeros((BM * SUB, LANES), F32)
        unowned = [row for lo, hi in unowned_rows for row in range(lo, hi)]
        assert len(unowned) <= BM
        for r, row in enumerate(unowned):
            _tile_copy(ybuf_ref.at[1], r, y2_ref, row, ssem).start(priority=r % 2)
        for r, row in enumerate(unowned):
            _tile_copy(ybuf_ref.at[1], r, y2_ref, row, ssem).wait()
        for r in range(BM):
            _tile_copy(ybuf_ref.at[1], r, y2_ref, dump_base + r, ssem).start(priority=r % 2)

    @pl.when(bfirst_ref[i] == 1)
    def _():
        wgb_ref[...] = wg_ref[...].astype(BF16)
        wub_ref[...] = wu_ref[...].astype(BF16)
        wdb_ref[...] = wd_ref[...].astype(BF16)

    def step(buf):
        x = _load_row_tiles(xs_ref, BM).astype(BF16)
        g = jnp.dot(x, wgb_ref[...], preferred_element_type=F32)
        u = jnp.dot(x, wub_ref[...], preferred_element_type=F32)
        a = (jax.nn.silu(g) * u).astype(BF16)
        y = jnp.dot(a, wdb_ref[...], preferred_element_type=F32)
        wait_scatter(1 - buf)
        _store_row_tiles(ybuf_ref.at[buf], y)
        start_scatter(buf)

        @pl.when(i == n_used - 1)
        def _():
            wait_scatter(buf)

    for buf in range(2):
        pl.when((i < n_used) & (i % 2 == buf))(functools.partial(step, buf))


def _expert_call(layer, blk_e, blk_first, n_used, slots, xs, w_gate, w_up, w_down,
                 t_all, choice_stride):
    n_blocks = blk_e.shape[0]
    dump_base = 2 * choice_stride
    n_y_rows = dump_base + BM
    unowned_rows = ((t_all, choice_stride), (choice_stride + t_all, 2 * choice_stride))
    wmap = lambda i, be, bf, nu: (layer, be[i], 0, 0)
    grid_spec = pltpu.PrefetchScalarGridSpec(
        num_scalar_prefetch=3,
        grid=(n_blocks,),
        in_specs=[
            pl.BlockSpec((None, 1, BM), lambda i, be, bf, nu: (jnp.minimum(i, nu[0] - 1), 0, 0),
                         memory_space=pltpu.SMEM),
            pl.BlockSpec((BM * SUB, LANES), lambda i, be, bf, nu: (jnp.minimum(i, nu[0] - 1), 0)),
            pl.BlockSpec((None, None, D_MODEL, D_EXPERT), wmap),
            pl.BlockSpec((None, None, D_MODEL, D_EXPERT), wmap),
            pl.BlockSpec((None, None, D_EXPERT, D_MODEL), wmap),
        ],
        out_specs=_ANY,
        scratch_shapes=[
            pltpu.VMEM((2, BM * SUB, LANES), F32),
            pltpu.VMEM((D_MODEL, D_EXPERT), BF16),
            pltpu.VMEM((D_MODEL, D_EXPERT), BF16),
            pltpu.VMEM((D_EXPERT, D_MODEL), BF16),
            pltpu.SemaphoreType.DMA,
        ],
    )
    return pl.pallas_call(
        functools.partial(_expert_kernel, dump_base=dump_base, unowned_rows=unowned_rows),
        grid_spec=grid_spec,
        out_shape=jax.ShapeDtypeStruct((n_y_rows * SUB, LANES), F32),
        name="expert_mlp",
        compiler_params=pltpu.CompilerParams(
            dimension_semantics=("arbitrary",),
            vmem_limit_bytes=VMEM_LIMIT),
    )(blk_e, blk_first, n_used, slots, xs, w_gate, w_up, w_down)


def _combine_kernel(x1_ref, route_ref, y0_ref, y1_ref, g2_ref, ln2g_ref, ln2b_ref, out_ref, *, tm):
    route = route_ref[...]
    f = route[:, 2:3] * _load_row_tiles(y0_ref, tm) + route[:, 3:4] * _load_row_tiles(y1_ref, tm)
    r = DEEPNORM_ALPHA * x1_ref[...] + (1.0 + g2_ref[...]) * f
    out_ref[...] = _layer_norm(r, ln2g_ref[...], ln2b_ref[...])


def _combine_call(x1_all, route_all, y2, g2, g2_spec, ln2_g, ln2_b, tm, row0, n_rows,
                  choice_stride, in_place):
    blk0 = row0 // tm
    blk1 = (choice_stride + row0) // tm
    assert row0 % tm == 0 and n_rows % tm == 0 and choice_stride % tm == 0
    rows = lambda i: (blk0 + i, 0)
    if in_place:
        out_spec = pl.BlockSpec((tm, D_MODEL), rows)
        out_shape = jax.ShapeDtypeStruct(x1_all.shape, F32)
        aliases = {0: 0}
    else:
        out_spec = pl.BlockSpec((tm, D_MODEL), lambda i: (i, 0))
        out_shape = jax.ShapeDtypeStruct((n_rows, D_MODEL), F32)
        aliases = {}
    return pl.pallas_call(
        functools.partial(_combine_kernel, tm=tm),
        grid=(n_rows // tm,),
        in_specs=[
            pl.BlockSpec((tm, D_MODEL), rows),
            pl.BlockSpec((tm, LANES), rows),
            pl.BlockSpec((tm * SUB, LANES), rows),
            pl.BlockSpec((tm * SUB, LANES), lambda i: (blk1 + i, 0)),
            g2_spec,
            _full((1, D_MODEL)),
            _full((1, D_MODEL)),
        ],
        out_specs=out_spec,
        out_shape=out_shape,
        input_output_aliases=aliases,
        name="combine_ln2",
        compiler_params=pltpu.CompilerParams(
            dimension_semantics=("arbitrary",),
            vmem_limit_bytes=VMEM_LIMIT),
    )(x1_all, route_all, y2, y2, g2, ln2_g, ln2_b)


def _split_bf16(w):
    hi = w.astype(BF16)
    lo = (w - hi.astype(F32)).astype(BF16)
    return jnp.concatenate([hi, lo], axis=1)


def _strict_lower(n):
    r = lax.broadcasted_iota(jnp.int32, (n, n), 0)
    c = lax.broadcasted_iota(jnp.int32, (n, n), 1)
    return (c < r).astype(BF16)


def kernel(x_prompt, x_sample, state_pool, c_prompt, c_sample, w_ada, b_ada, w_in, pool_w, pool_scale, sgu_norm_g, sgu_w, sgu_b, w_out, ln1_g, ln1_b, router_g_w, router_g_b, router_e_w, router_e_b, exp_w_gate, exp_w_up, exp_w_down, ln2_g, ln2_b):
    n_seq, seq_len, _ = x_prompt.shape
    n_dec = x_sample.shape[0]
    t_p = n_seq * seq_len
    t_all = t_p + n_dec
    assert t_p % n_dec == 0 and seq_len % TM_MIX == 0 and seq_len % TM_ROW == 0
    assert t_all % MAP_STEPS == 0 and t_all % TM_DISPATCH == 0
    n_assign = 2 * t_all
    n_blocks = -(-n_assign // BM) + N_EXPERTS
    n_slots = n_blocks * BM
    choice_stride = -(-t_all // TM_ROW) * TM_ROW
    dump_base = 2 * choice_stride
    n_y_rows = dump_base + BM
    assert n_y_rows <= (1 << (32 - ROW_BITS)) and t_all <= (1 << ROW_BITS)

    x_all = jnp.concatenate([x_prompt.reshape(t_p, D_MODEL), x_sample.reshape(n_dec, D_MODEL)])
    mod = _mod_call(jnp.concatenate([c_prompt, c_sample], axis=0), w_ada, b_ada)
    ltri_p = _strict_lower(TM_MIX)
    ltri_s = _strict_lower(n_dec)
    lane_pad = jnp.zeros((D_MODEL, LANES - N_EXPERTS - N_GROUPS), F32)
    expert_ids = jnp.arange(N_EXPERTS, dtype=jnp.int32)
    block_start = jnp.arange(n_blocks, dtype=jnp.int32) * BM
    fill_words = (dump_base + (jnp.arange(n_slots, dtype=jnp.int32) & (BM - 1))) << ROW_BITS

    h2_all = jnp.zeros((t_all * SUB, LANES), F32)
    route_all = jnp.zeros((t_all, LANES), F32)

    pool_p, pool_s, v_s = [], [], []
    for l in range(DEPTH):
        lw = {
            "w_in": w_in[l].astype(BF16),
            "w_out": w_out[l].astype(BF16),
            "pool_w": pool_w[l].astype(BF16),
            "pool_scale": pool_scale[l].reshape(1, D_POOL),
            "sgu_g": sgu_norm_g[l].reshape(1, D_SGU),
            "sgu_w": sgu_w[l],
            "sgu_b_t": sgu_b[l].T,
            "sgu_w0": jnp.repeat(sgu_w[l, :, 0, 0], SGU_HEAD_DIM).reshape(1, D_SGU),
            "sgu_b0": jnp.repeat(sgu_b[l, :, 0], SGU_HEAD_DIM).reshape(1, D_SGU),
            "ln1_g": ln1_g[l].reshape(1, D_MODEL),
            "ln1_b": ln1_b[l].reshape(1, D_MODEL),
            "w_r": _split_bf16(jnp.concatenate([router_e_w[l], router_g_w[l], lane_pad], axis=1)),
            "b_r": jnp.concatenate([router_e_b[l], router_g_b[l],
                                    jnp.zeros((LANES - N_EXPERTS - N_GROUPS,), F32)]).reshape(1, LANES),
            "ltri_p": ltri_p,
            "ltri_s": ltri_s,
        }
        mod_p = mod[l, :n_seq].reshape(n_seq, 6, D_MODEL)
        mod_s = mod[l, n_seq:]

        x1_all, h2_all, route_all, nbuf_p, cnt_p = _mix_prompt_call(
            x_all, h2_all, route_all, mod_p, lw, n_seq, seq_len)
        hist = jnp.transpose(state_pool[l], (1, 0, 2))
        x1_all, h2_all, route_all, nbuf_s, vn_s, cnt = _mix_sample_call(
            x1_all, h2_all, route_all, mod_s, hist, lw, cnt_p, n_dec, t_p)
        pool_p.append(nbuf_p[:, HIST - POOL_BUF:])
        pool_s.append(jnp.transpose(nbuf_s, (1, 0, 2)))
        v_s.append(vn_s.reshape(n_dec, 1, D_SGU))

        counts = cnt[0, :N_EXPERTS].astype(jnp.int32)
        padded = ((counts + BM - 1) // BM) * BM
        pends = jnp.cumsum(padded)
        pstarts = pends - padded
        e_sel = route_all[:, 0:2].astype(jnp.int32)
        rank = route_all[:, 4:6].astype(jnp.int32)
        dest = rank + jnp.sum(jnp.where(e_sel[..., None] == expert_ids, pstarts, 0), axis=-1)
        slots = _slot_map_call(dest.T.reshape(n_assign), fill_words, choice_stride)
        slots = slots.reshape(n_blocks, 1, BM)

        blk_valid = block_start < pends[-1]
        n_used = (pends[-1:] // BM).astype(jnp.int32)
        blk_e = jnp.sum((pends[None, :] <= block_start[:, None]).astype(jnp.int32), axis=1)
        blk_e = jnp.minimum(blk_e, N_EXPERTS - 1)
        last_e = jnp.max(jnp.where(blk_valid, blk_e, 0))
        blk_e = jnp.where(blk_valid, blk_e, last_e).astype(jnp.int32)
        blk_first = (blk_valid & jnp.any(pstarts[None, :] == block_start[:, None], axis=1)
                     ).astype(jnp.int32)

        xs = _dispatch_call(pends, dest, h2_all, n_blocks, n_assign // BM)
        y2 = _expert_call(l, blk_e, blk_first, n_used, slots, xs,
                          exp_w_gate, exp_w_up, exp_w_down, t_all, choice_stride)

        ln2g = ln2_g[l].reshape(1, D_MODEL)
        ln2b = ln2_b[l].reshape(1, D_MODEL)
        tiles_per_seq = seq_len // TM_ROW
        g2_p_spec = pl.BlockSpec((None, 1, D_MODEL), lambda i: (i // tiles_per_seq, 0, 0))
        last = l == DEPTH - 1
        out_p = _combine_call(x1_all, route_all, y2, mod_p[:, 5:6, :], g2_p_spec, ln2g, ln2b,
                              TM_ROW, 0, t_p, choice_stride, in_place=not last)
        out_s = _combine_call(x1_all if last else out_p, route_all, y2, mod_s[:, 5 * D_MODEL:],
                              _full((n_dec, D_MODEL)), ln2g, ln2b,
                              n_dec, t_p, n_dec, choice_stride, in_place=not last)
        x_all = out_s

    return (out_p.reshape(n_seq, seq_len, D_MODEL),
            out_s.reshape(n_dec, 1, D_MODEL),
            jnp.stack(pool_p, axis=0),
            jnp.stack(pool_s, axis=0),
            jnp.stack(v_s, axis=0))
```

```python
import functools

import jax
import jax.numpy as jnp
from jax import lax
from jax.experimental import pallas as pl
from jax.experimental.pallas import tpu as pltpu

D_MODEL = 1024
DEPTH = 4
PAST_LEN = 16384
D_POOL = 512
N_POOL_GROUPS = 4
POOL_GROUP_DIM = 128
POOL_WINDOWS = (2, 4, 8, 16)
POOL_BUF = 15
HIST = 16
D_SGU = 512
SGU_HEADS = 4
SGU_HEAD_DIM = 128
CHUNK = 128
N_GROUPS = 4
EXPERTS_PER_GROUP = 8
N_EXPERTS = 32
D_EXPERT = 512
DEEPNORM_ALPHA = (2.0 * DEPTH) ** 0.25
LN_EPS = 1e-5

LANES = 128
SUB = 8
assert D_MODEL == SUB * LANES
ROUTE_G_LANE = N_EXPERTS
SECOND_LANE = 64

TM_MIX = 512
TM_ROW = 256
TM_DISPATCH = 384
BM = 256
ROW_BITS = 16

F32 = jnp.float32
BF16 = jnp.bfloat16
VMEM_LIMIT = 48 * 1024 * 1024


def _layer_norm(r, g, b):
    mu = jnp.mean(r, axis=-1, keepdims=True)
    rc = r - mu
    var = jnp.mean(rc * rc, axis=-1, keepdims=True)
    return rc * lax.rsqrt(var + LN_EPS) * g + b


def _head_norm(vh, g):
    mu = jnp.mean(vh, axis=-1, keepdims=True)
    vc = vh - mu
    var = jnp.mean(vc * vc, axis=-1, keepdims=True)
    return vc * lax.rsqrt(var + LN_EPS) * g


def _roll_half(row):
    return pltpu.roll(jnp.broadcast_to(row, (8, LANES)), SECOND_LANE, 1)[0:1]


def _store_row_tiles(ref, val):
    rows = val.shape[0]
    for c in range(SUB):
        ref[pl.ds(c, rows, stride=SUB), :] = val[:, c * LANES:(c + 1) * LANES]


def _load_row_tiles(ref, rows):
    return jnp.concatenate(
        [ref[pl.ds(c, rows, stride=SUB), :] for c in range(SUB)], axis=1)


def _route(logits, ltri_ref, run_ref):
    tm = logits.shape[0]
    lane = lax.broadcasted_iota(jnp.int32, (tm, LANES), 1)
    lanef = lane.astype(F32)
    neg = -jnp.inf
    is_g = (lane >= ROUTE_G_LANE) & (lane < ROUTE_G_LANE + N_GROUPS)
    glm = jnp.where(is_g, logits, neg)
    gmax = jnp.max(glm, axis=1, keepdims=True)
    g_idx = jnp.min(jnp.where(glm == gmax, lanef - ROUTE_G_LANE, 1e4), axis=1, keepdims=True)
    p_g = 1.0 / jnp.sum(jnp.exp(glm - gmax), axis=1, keepdims=True)

    in_grp = (lane < N_EXPERTS) & ((lane >> 3).astype(F32) == g_idx)
    elm = jnp.where(in_grp, logits, neg)
    m1 = jnp.max(elm, axis=1, keepdims=True)
    i1 = jnp.min(jnp.where(elm == m1, lanef, 1e4), axis=1, keepdims=True)
    elm2 = jnp.where(lanef == i1, neg, elm)
    m2 = jnp.max(elm2, axis=1, keepdims=True)
    i2 = jnp.min(jnp.where(elm2 == m2, lanef, 1e4), axis=1, keepdims=True)
    e21 = jnp.exp(m2 - m1)
    den = 1.0 + e21
    w0 = (1.0 / den) * p_g
    w1 = (e21 / den) * p_g

    hit0 = lanef == i1
    hit1 = lanef == i2 + SECOND_LANE
    oh = jnp.where(hit0 | hit1, 1.0, 0.0)
    before = jnp.dot(ltri_ref[...], oh.astype(BF16), preferred_element_type=F32)
    tot = jnp.sum(oh, axis=0, keepdims=True)
    lane1 = lax.broadcasted_iota(jnp.int32, (1, LANES), 1)
    tot0 = jnp.where(lane1 < N_EXPERTS, tot, 0.0)
    tot1 = jnp.where(lane1 >= SECOND_LANE, tot, 0.0)
    run = run_ref[...]
    base = run + _roll_half(run + tot0)
    val = before + base
    rank0 = jnp.sum(jnp.where(hit0, val, 0.0), axis=1, keepdims=True)
    rank1 = jnp.sum(jnp.where(hit1, val, 0.0), axis=1, keepdims=True)
    run_ref[...] = run + tot0 + _roll_half(tot1)

    out = jnp.where(lane == 0, i1, 0.0)
    out = jnp.where(lane == 1, i2, out)
    out = jnp.where(lane == 2, w0, out)
    out = jnp.where(lane == 3, w1, out)
    out = jnp.where(lane == 4, rank0, out)
    out = jnp.where(lane == 5, rank1, out)
    return out


def _finish_mix(x, mix, g1, sh2, sc2, ln1g_ref, ln1b_ref, wr_ref, br_ref, ltri_ref, run_ref,
                x1_ref, h2_ref, route_ref):
    r = DEEPNORM_ALPHA * x + (1.0 + g1) * mix
    x1 = _layer_norm(r, ln1g_ref[...], ln1b_ref[...])
    x1_ref[...] = x1
    h2 = x1 * (1.0 + sc2) + sh2
    _store_row_tiles(h2_ref, h2)
    hi = h2.astype(BF16)
    lo = (h2 - hi.astype(F32)).astype(BF16)
    both = jnp.dot(hi, wr_ref[...], preferred_element_type=F32)
    logits = (both[:, :LANES] + both[:, LANES:]
              + jnp.dot(lo, wr_ref[:, :LANES], preferred_element_type=F32) + br_ref[...])
    route_ref[...] = _route(logits, ltri_ref, run_ref)


def _masked_sgu_w(sguw_ref, hd):
    t = lax.broadcasted_iota(jnp.int32, (CHUNK, CHUNK), 0)
    s = lax.broadcasted_iota(jnp.int32, (CHUNK, CHUNK), 1)
    return jnp.where(s <= t, sguw_ref[hd], 0.0).astype(BF16)


def _full(shape):
    nd = len(shape)
    return pl.BlockSpec(shape, lambda *_: (0,) * nd)


_ANY = pl.BlockSpec(memory_space=pl.ANY)


def _mod_kernel(c_ref, w_ref, b_ref, o_ref):
    s = jax.nn.silu(c_ref[...])
    o_ref[...] = jnp.dot(s.astype(BF16), w_ref[...].astype(BF16),
                         preferred_element_type=F32) + b_ref[...]


def _mod_call(c_all, w_ada, b_ada):
    n = c_all.shape[0]
    tn = 1024
    return pl.pallas_call(
        _mod_kernel,
        grid=(DEPTH, 6 * D_MODEL // tn),
        in_specs=[
            pl.BlockSpec((n, D_MODEL), lambda l, j: (0, 0)),
            pl.BlockSpec((None, D_MODEL, tn), lambda l, j: (l, 0, j)),
            pl.BlockSpec((None, 1, tn), lambda l, j: (l, 0, j)),
        ],
        out_specs=pl.BlockSpec((None, n, tn), lambda l, j: (l, 0, j)),
        out_shape=jax.ShapeDtypeStruct((DEPTH, n, 6 * D_MODEL), F32),
        name="adaln_mod",
        compiler_params=pltpu.CompilerParams(
            dimension_semantics=("arbitrary", "arbitrary")),
    )(c_all, w_ada, b_ada.reshape(DEPTH, 1, 6 * D_MODEL))


def _mix_prompt_kernel(x_ref, mod_ref, win_ref, wout_ref, poolw_ref, pscale_ref, sgug_ref,
                       sguw_ref, sgub_ref, ln1g_ref, ln1b_ref, wr_ref, br_ref, ltri_ref,
                       h2_in_ref, route_in_ref,
                       x1_ref, h2_ref, route_ref, nbuf_ref, cnt_ref,
                       pe_ref, mixin_ref, run_ref, *, tm, tiles_per_seq):
    del h2_in_ref, route_in_ref
    b = pl.program_id(0)
    j = pl.program_id(1)

    @pl.when((b == 0) & (j == 0))
    def _():
        run_ref[...] = jnp.zeros_like(run_ref)

    @pl.when(j == 0)
    def _():
        pe_ref[0:HIST, :] = jnp.zeros((HIST, D_POOL), F32)

    x = x_ref[...]
    mod = mod_ref[...]
    sh1, sc1, g1, sh2, sc2 = (mod[i:i + 1] for i in range(5))
    h = x * (1.0 + sc1) + sh1
    proj = jnp.dot(h.astype(BF16), win_ref[...], preferred_element_type=F32)
    p = proj[:, :D_POOL]
    u = proj[:, D_POOL:D_POOL + D_SGU]
    v = proj[:, D_POOL + D_SGU:]

    pe_ref[HIST:HIST + tm, :] = p
    pos = j * tm + lax.broadcasted_iota(jnp.int32, (tm, 1), 0)
    for g, w in enumerate(POOL_WINDOWS):
        lo, hi = g * POOL_GROUP_DIM, (g + 1) * POOL_GROUP_DIM
        pg = p[:, lo:hi]
        s = pg
        for k in range(1, w):
            s = s + pe_ref[pl.ds(HIST - k, tm), lo:hi]
        cnt = jnp.minimum(pos + 1, w).astype(F32)
        d = s / cnt - pg
        a = jnp.dot(d.astype(BF16), poolw_ref[g], preferred_element_type=F32)
        mixin_ref[:, lo:hi] = (a * pscale_ref[:, lo:hi]).astype(BF16)

    @pl.when(j == tiles_per_seq - 1)
    def _():
        nbuf_ref[...] = pe_ref[tm:tm + HIST, :]

    pe_ref[0:HIST, :] = pe_ref[tm:tm + HIST, :]

    for hd in range(SGU_HEADS):
        lo, hi = hd * SGU_HEAD_DIM, (hd + 1) * SGU_HEAD_DIM
        vn = _head_norm(v[:, lo:hi], sgug_ref[:, lo:hi]).astype(BF16)
        ws = _masked_sgu_w(sguw_ref, hd)
        bcol = sgub_ref[:, hd:hd + 1]
        for c in range(tm // CHUNK):
            r0, r1 = c * CHUNK, (c + 1) * CHUNK
            z = jnp.dot(ws, vn[r0:r1], preferred_element_type=F32) + bcol
            mixin_ref[r0:r1, D_POOL + lo:D_POOL + hi] = (u[r0:r1, lo:hi] * z).astype(BF16)

    mix = jnp.dot(mixin_ref[...], wout_ref[...], preferred_element_type=F32)
    _finish_mix(x, mix, g1, sh2, sc2, ln1g_ref, ln1b_ref, wr_ref, br_ref, ltri_ref, run_ref,
                x1_ref, h2_ref, route_ref)
    cnt_ref[...] = run_ref[...]


def _mix_prompt_call(x_all, h2_buf, route_buf, mod_p, lw, n_seq, seq_len):
    tm = TM_MIX
    tps = seq_len // tm
    t_all = x_all.shape[0]
    row = lambda b, j: (b * tps + j, 0)
    return pl.pallas_call(
        functools.partial(_mix_prompt_kernel, tm=tm, tiles_per_seq=tps),
        grid=(n_seq, tps),
        in_specs=[
            pl.BlockSpec((tm, D_MODEL), row),
            pl.BlockSpec((None, 6, D_MODEL), lambda b, j: (b, 0, 0)),
            _full((D_MODEL, D_POOL + 2 * D_SGU)),
            _full((D_POOL + D_SGU, D_MODEL)),
            _full((N_POOL_GROUPS, POOL_GROUP_DIM, POOL_GROUP_DIM)),
            _full((1, D_POOL)),
            _full((1, D_SGU)),
            _full((SGU_HEADS, CHUNK, CHUNK)),
            _full((CHUNK, SGU_HEADS)),
            _full((1, D_MODEL)),
            _full((1, D_MODEL)),
            _full((D_MODEL, 2 * LANES)),
            _full((1, LANES)),
            _full((tm, tm)),
            _ANY, _ANY,
        ],
        out_specs=[
            pl.BlockSpec((tm, D_MODEL), row),
            pl.BlockSpec((tm * SUB, LANES), row),
            pl.BlockSpec((tm, LANES), row),
            pl.BlockSpec((None, HIST, D_POOL), lambda b, j: (b, 0, 0)),
            pl.BlockSpec((1, LANES), lambda b, j: (0, 0)),
        ],
        out_shape=[
            jax.ShapeDtypeStruct((t_all, D_MODEL), F32),
            jax.ShapeDtypeStruct((t_all * SUB, LANES), F32),
            jax.ShapeDtypeStruct((t_all, LANES), F32),
            jax.ShapeDtypeStruct((n_seq, HIST, D_POOL), F32),
            jax.ShapeDtypeStruct((1, LANES), F32),
        ],
        scratch_shapes=[
            pltpu.VMEM((HIST + tm, D_POOL), F32),
            pltpu.VMEM((tm, D_POOL + D_SGU), BF16),
            pltpu.VMEM((1, LANES), F32),
        ],
        input_output_aliases={0: 0, 14: 1, 15: 2},
        name="mix_prompt",
        compiler_params=pltpu.CompilerParams(
            dimension_semantics=("arbitrary", "arbitrary"),
            vmem_limit_bytes=VMEM_LIMIT),
    )(x_all, mod_p, lw["w_in"], lw["w_out"], lw["pool_w"], lw["pool_scale"], lw["sgu_g"],
      lw["sgu_w"], lw["sgu_b_t"], lw["ln1_g"], lw["ln1_b"], lw["w_r"], lw["b_r"], lw["ltri_p"],
      h2_buf, route_buf)


def _mix_sample_kernel(x_ref, mod_ref, hist_ref, win_ref, wout_ref, poolw_ref, pscale_ref,
                       sgug_ref, sguw0_ref, sgub0_ref, ln1g_ref, ln1b_ref, wr_ref, br_ref,
                       ltri_ref, cnt_in_ref, h2_in_ref, route_in_ref,
                       x1_ref, h2_ref, route_ref, nbuf_ref, vn_ref, cnt_ref,
                       mixin_ref, run_ref):
    del h2_in_ref, route_in_ref
    run_ref[...] = cnt_in_ref[...]
    x = x_ref[...]
    sh1, sc1, g1, sh2, sc2 = (mod_ref[:, i * D_MODEL:(i + 1) * D_MODEL] for i in range(5))
    h = x * (1.0 + sc1) + sh1
    proj = jnp.dot(h.astype(BF16), win_ref[...], preferred_element_type=F32)
    p = proj[:, :D_POOL]
    u = proj[:, D_POOL:D_POOL + D_SGU]
    v = proj[:, D_POOL + D_SGU:]

    for g, w in enumerate(POOL_WINDOWS):
        lo, hi = g * POOL_GROUP_DIM, (g + 1) * POOL_GROUP_DIM
        pg = p[:, lo:hi]
        s = pg
        for k in range(1, w):
            s = s + hist_ref[POOL_BUF - k, :, lo:hi]
        cnt = float(min(PAST_LEN + 1, w))
        d = s / cnt - pg
        a = jnp.dot(d.astype(BF16), poolw_ref[g], preferred_element_type=F32)
        mixin_ref[:, lo:hi] = (a * pscale_ref[:, lo:hi]).astype(BF16)

    for k in range(POOL_BUF - 1):
        nbuf_ref[k] = hist_ref[k + 1]
    nbuf_ref[POOL_BUF - 1] = p

    for hd in range(SGU_HEADS):
        lo, hi = hd * SGU_HEAD_DIM, (hd + 1) * SGU_HEAD_DIM
        vn = _head_norm(v[:, lo:hi], sgug_ref[:, lo:hi])
        vn_ref[:, lo:hi] = vn
        z = sguw0_ref[:, lo:hi].astype(BF16).astype(F32) * vn.astype(BF16).astype(F32) \
            + sgub0_ref[:, lo:hi]
        mixin_ref[:, D_POOL + lo:D_POOL + hi] = (u[:, lo:hi] * z).astype(BF16)

    mix = jnp.dot(mixin_ref[...], wout_ref[...], preferred_element_type=F32)
    _finish_mix(x, mix, g1, sh2, sc2, ln1g_ref, ln1b_ref, wr_ref, br_ref, ltri_ref, run_ref,
                x1_ref, h2_ref, route_ref)
    cnt_ref[...] = run_ref[...]


def _mix_sample_call(x1_all, h2_all, route_all, mod_s, hist, lw, cnt_in, n_dec, row0):
    blk = row0 // n_dec
    rows = lambda r, w: pl.BlockSpec((r, w), lambda i: (blk, 0))
    return pl.pallas_call(
        _mix_sample_kernel,
        grid=(1,),
        in_specs=[
            rows(n_dec, D_MODEL),
            _full((n_dec, 6 * D_MODEL)),
            _full((POOL_BUF, n_dec, D_POOL)),
            _full((D_MODEL, D_POOL + 2 * D_SGU)),
            _full((D_POOL + D_SGU, D_MODEL)),
            _full((N_POOL_GROUPS, POOL_GROUP_DIM, POOL_GROUP_DIM)),
            _full((1, D_POOL)),
            _full((1, D_SGU)),
            _full((1, D_SGU)),
            _full((1, D_SGU)),
            _full((1, D_MODEL)),
            _full((1, D_MODEL)),
            _full((D_MODEL, 2 * LANES)),
            _full((1, LANES)),
            _full((n_dec, n_dec)),
            _full((1, LANES)),
            _ANY, _ANY,
        ],
        out_specs=[
            rows(n_dec, D_MODEL),
            rows(n_dec * SUB, LANES),
            rows(n_dec, LANES),
            _full((POOL_BUF, n_dec, D_POOL)),
            _full((n_dec, D_SGU)),
            _full((1, LANES)),
        ],
        out_shape=[
            jax.ShapeDtypeStruct(x1_all.shape, F32),
            jax.ShapeDtypeStruct(h2_all.shape, F32),
            jax.ShapeDtypeStruct(route_all.shape, F32),
            jax.ShapeDtypeStruct((POOL_BUF, n_dec, D_POOL), F32),
            jax.ShapeDtypeStruct((n_dec, D_SGU), F32),
            jax.ShapeDtypeStruct((1, LANES), F32),
        ],
        scratch_shapes=[
            pltpu.VMEM((n_dec, D_POOL + D_SGU), BF16),
            pltpu.VMEM((1, LANES), F32),
        ],
        input_output_aliases={0: 0, 16: 1, 17: 2},
        name="mix_sample",
        compiler_params=pltpu.CompilerParams(
            dimension_semantics=("arbitrary",),
            vmem_limit_bytes=VMEM_LIMIT),
    )(x1_all, mod_s, hist, lw["w_in"], lw["w_out"], lw["pool_w"], lw["pool_scale"], lw["sgu_g"],
      lw["sgu_w0"], lw["sgu_b0"], lw["ln1_g"], lw["ln1_b"], lw["w_r"], lw["b_r"], lw["ltri_s"],
      cnt_in, h2_all, route_all)


def _tile_copy(src_ref, src_row, dst_ref, dst_row, sem):
    def tile(ref, row):
        start = row * SUB if isinstance(row, int) else pl.multiple_of(row * SUB, SUB)
        return ref.at[pl.ds(start, SUB), :]

    return pltpu.make_async_copy(tile(src_ref, src_row), tile(dst_ref, dst_row), sem)


def _dispatch_kernel(pends_ref, dest_ref, h2_ref, fill_ref, xs_ref, slots_ref, zero_ref, sem,
                     *, tm, n_blocks, min_used, choice_stride):
    i = pl.program_id(0)

    @pl.when(i == 0)
    def _():
        fill = pltpu.make_async_copy(fill_ref, slots_ref, sem)
        fill.start()
        fill.wait()
        zero_ref[...] = jnp.zeros_like(zero_ref)
        n_used = lax.shift_right_logical(pends_ref[N_EXPERTS - 1], BM.bit_length() - 1)

        def block_copy(b):
            first = pl.multiple_of(b * (BM * SUB), BM * SUB)
            return pltpu.make_async_copy(zero_ref, xs_ref.at[pl.ds(first, BM * SUB), :], sem)

        def last_block(e):
            return lax.shift_right_logical(pends_ref[e], BM.bit_length() - 1) - 1

        for act in ("start", "wait"):
            for e in range(N_EXPERTS):
                prev_end = pends_ref[e - 1] if e else 0
                pl.when(pends_ref[e] > prev_end)(
                    lambda e=e, act=act: getattr(block_copy(last_block(e)), act)())
            for b in range(min_used, n_blocks):
                pl.when(b >= n_used)(lambda b=b, act=act: getattr(block_copy(b), act)())

    tile_word = (i * tm) << ROW_BITS
    for r in range(tm):
        for k in range(2):
            slot = dest_ref[0, 2 * r + k]
            _tile_copy(h2_ref, r, xs_ref, slot, sem).start(priority=k)
            slots_ref[slot] = tile_word + ((k * choice_stride + r) << ROW_BITS)
    for r in range(tm):
        for k in range(2):
            _tile_copy(h2_ref, r, xs_ref, 0, sem).wait()


def _dispatch_call(pends, dest, h2_all, fill_words, n_blocks, min_used, choice_stride):
    tm = TM_DISPATCH
    n_tiles = dest.shape[0] // tm
    grid_spec = pltpu.PrefetchScalarGridSpec(
        num_scalar_prefetch=1,
        grid=(n_tiles,),
        in_specs=[
            pl.BlockSpec((None, 1, 2 * tm), lambda i, pe: (i, 0, 0), memory_space=pltpu.SMEM),
            pl.BlockSpec((tm * SUB, LANES), lambda i, pe: (i, 0)),
            _ANY,
        ],
        out_specs=[_ANY, pl.BlockSpec(memory_space=pltpu.SMEM)],
        scratch_shapes=[pltpu.VMEM((BM * SUB, LANES), F32), pltpu.SemaphoreType.DMA],
    )
    return pl.pallas_call(
        functools.partial(_dispatch_kernel, tm=tm, n_blocks=n_blocks, min_used=min_used,
                          choice_stride=choice_stride),
        grid_spec=grid_spec,
        out_shape=[jax.ShapeDtypeStruct((n_blocks * BM * SUB, LANES), F32),
                   jax.ShapeDtypeStruct(fill_words.shape, jnp.int32)],
        name="dispatch_rows",
        compiler_params=pltpu.CompilerParams(dimension_semantics=("arbitrary",)),
    )(pends, dest.reshape(n_tiles, 1, 2 * tm), h2_all, fill_words)


def _expert_kernel(be_ref, bfirst_ref, nused_ref, slot_ref, xs_ref, wg_ref, wu_ref, wd_ref,
                   y2_ref, ybuf_ref, wgb_ref, wub_ref, wdb_ref, ssem,
                   *, dump_base, unowned_rows):
    del be_ref
    i = pl.program_id(0)
    n_used = nused_ref[0]

    def start_scatter(buf):
        for r in range(BM):
            row = lax.shift_right_logical(slot_ref[0, r], ROW_BITS)
            _tile_copy(ybuf_ref.at[buf], r, y2_ref, row, ssem.at[buf]).start(priority=r % 2)

    def wait_scatter(buf):
        for r in range(BM):
            _tile_copy(ybuf_ref.at[buf], r, y2_ref, 0, ssem.at[buf]).wait()

    @pl.when(i == 0)
    def _():
        ybuf_ref[...] = jnp.zeros_like(ybuf_ref)
        unowned = [row for lo, hi in unowned_rows for row in range(lo, hi)]
        assert len(unowned) <= BM
        for r, row in enumerate(unowned):
            _tile_copy(ybuf_ref.at[1], r, y2_ref, row, ssem.at[1]).start(priority=r % 2)
        for r, row in enumerate(unowned):
            _tile_copy(ybuf_ref.at[1], r, y2_ref, row, ssem.at[1]).wait()
        for buf in range(2):
            for r in range(BM):
                _tile_copy(ybuf_ref.at[buf], r, y2_ref, dump_base + buf * BM + r,
                           ssem.at[buf]).start(priority=r % 2)

    @pl.when(bfirst_ref[i] == 1)
    def _():
        wgb_ref[...] = wg_ref[...].astype(BF16)
        wub_ref[...] = wu_ref[...].astype(BF16)
        wdb_ref[...] = wd_ref[...].astype(BF16)

    def step(buf):
        x = _load_row_tiles(xs_ref, BM).astype(BF16)
        g = jnp.dot(x, wgb_ref[...], preferred_element_type=F32)
        u = jnp.dot(x, wub_ref[...], preferred_element_type=F32)
        a = (jax.nn.silu(g) * u).astype(BF16)
        y = jnp.dot(a, wdb_ref[...], preferred_element_type=F32)
        wait_scatter(buf)
        _store_row_tiles(ybuf_ref.at[buf], y)
        start_scatter(buf)

        @pl.when(i == n_used - 1)
        def _():
            wait_scatter(1 - buf)
            wait_scatter(buf)

    for buf in range(2):
        pl.when((i < n_used) & (i % 2 == buf))(functools.partial(step, buf))


def _expert_call(layer, blk_e, blk_first, n_used, slots, xs, w_gate, w_up, w_down,
                 t_all, choice_stride):
    n_blocks = blk_e.shape[0]
    dump_base = 2 * choice_stride
    n_y_rows = dump_base + 2 * BM
    unowned_rows = ((t_all, choice_stride), (choice_stride + t_all, 2 * choice_stride))
    wmap = lambda i, be, bf, nu: (layer, be[i], 0, 0)
    grid_spec = pltpu.PrefetchScalarGridSpec(
        num_scalar_prefetch=3,
        grid=(n_blocks,),
        in_specs=[
            pl.BlockSpec((None, 1, BM), lambda i, be, bf, nu: (jnp.minimum(i, nu[0] - 1), 0, 0),
                         memory_space=pltpu.SMEM),
            pl.BlockSpec((BM * SUB, LANES), lambda i, be, bf, nu: (jnp.minimum(i, nu[0] - 1), 0)),
            pl.BlockSpec((None, None, D_MODEL, D_EXPERT), wmap),
            pl.BlockSpec((None, None, D_MODEL, D_EXPERT), wmap),
            pl.BlockSpec((None, None, D_EXPERT, D_MODEL), wmap),
        ],
        out_specs=_ANY,
        scratch_shapes=[
            pltpu.VMEM((2, BM * SUB, LANES), F32),
            pltpu.VMEM((D_MODEL, D_EXPERT), BF16),
            pltpu.VMEM((D_MODEL, D_EXPERT), BF16),
            pltpu.VMEM((D_EXPERT, D_MODEL), BF16),
            pltpu.SemaphoreType.DMA((2,)),
        ],
    )
    return pl.pallas_call(
        functools.partial(_expert_kernel, dump_base=dump_base, unowned_rows=unowned_rows),
        grid_spec=grid_spec,
        out_shape=jax.ShapeDtypeStruct((n_y_rows * SUB, LANES), F32),
        name="expert_mlp",
        compiler_params=pltpu.CompilerParams(
            dimension_semantics=("arbitrary",),
            vmem_limit_bytes=VMEM_LIMIT),
    )(blk_e, blk_first, n_used, slots, xs, w_gate, w_up, w_down)


def _combine_kernel(x1_ref, route_ref, y0_ref, y1_ref, g2_ref, ln2g_ref, ln2b_ref, out_ref, *, tm):
    route = route_ref[...]
    f = route[:, 2:3] * _load_row_tiles(y0_ref, tm) + route[:, 3:4] * _load_row_tiles(y1_ref, tm)
    r = DEEPNORM_ALPHA * x1_ref[...] + (1.0 + g2_ref[...]) * f
    out_ref[...] = _layer_norm(r, ln2g_ref[...], ln2b_ref[...])


def _combine_call(x1_all, route_all, y2, g2, g2_spec, ln2_g, ln2_b, tm, row0, n_rows,
                  choice_stride, in_place):
    blk0 = row0 // tm
    blk1 = (choice_stride + row0) // tm
    assert row0 % tm == 0 and n_rows % tm == 0 and choice_stride % tm == 0
    rows = lambda i: (blk0 + i, 0)
    if in_place:
        out_spec = pl.BlockSpec((tm, D_MODEL), rows)
        out_shape = jax.ShapeDtypeStruct(x1_all.shape, F32)
        aliases = {0: 0}
    else:
        out_spec = pl.BlockSpec((tm, D_MODEL), lambda i: (i, 0))
        out_shape = jax.ShapeDtypeStruct((n_rows, D_MODEL), F32)
        aliases = {}
    return pl.pallas_call(
        functools.partial(_combine_kernel, tm=tm),
        grid=(n_rows // tm,),
        in_specs=[
            pl.BlockSpec((tm, D_MODEL), rows),
            pl.BlockSpec((tm, LANES), rows),
            pl.BlockSpec((tm * SUB, LANES), rows),
            pl.BlockSpec((tm * SUB, LANES), lambda i: (blk1 + i, 0)),
            g2_spec,
            _full((1, D_MODEL)),
            _full((1, D_MODEL)),
        ],
        out_specs=out_spec,
        out_shape=out_shape,
        input_output_aliases=aliases,
        name="combine_ln2",
        compiler_params=pltpu.CompilerParams(
            dimension_semantics=("arbitrary",),
            vmem_limit_bytes=VMEM_LIMIT),
    )(x1_all, route_all, y2, y2, g2, ln2_g, ln2_b)


def _split_bf16(w):
    hi = w.astype(BF16)
    lo = (w - hi.astype(F32)).astype(BF16)
    return jnp.concatenate([hi, lo], axis=1)


def _strict_lower(n):
    r = lax.broadcasted_iota(jnp.int32, (n, n), 0)
    c = lax.broadcasted_iota(jnp.int32, (n, n), 1)
    return (c < r).astype(BF16)


def kernel(x_prompt, x_sample, state_pool, c_prompt, c_sample, w_ada, b_ada, w_in, pool_w, pool_scale, sgu_norm_g, sgu_w, sgu_b, w_out, ln1_g, ln1_b, router_g_w, router_g_b, router_e_w, router_e_b, exp_w_gate, exp_w_up, exp_w_down, ln2_g, ln2_b):
    n_seq, seq_len, _ = x_prompt.shape
    n_dec = x_sample.shape[0]
    t_p = n_seq * seq_len
    t_all = t_p + n_dec
    assert t_p % n_dec == 0 and seq_len % TM_MIX == 0 and seq_len % TM_ROW == 0
    assert t_all % TM_DISPATCH == 0
    n_assign = 2 * t_all
    n_blocks = -(-n_assign // BM) + N_EXPERTS
    n_slots = n_blocks * BM
    choice_stride = -(-t_all // TM_ROW) * TM_ROW
    dump_base = 2 * choice_stride
    n_y_rows = dump_base + 2 * BM
    assert n_y_rows <= (1 << (32 - ROW_BITS)) and t_all <= (1 << ROW_BITS)

    x_all = jnp.concatenate([x_prompt.reshape(t_p, D_MODEL), x_sample.reshape(n_dec, D_MODEL)])
    mod = _mod_call(jnp.concatenate([c_prompt, c_sample], axis=0), w_ada, b_ada)
    ltri_p = _strict_lower(TM_MIX)
    ltri_s = _strict_lower(n_dec)
    lane_pad = jnp.zeros((D_MODEL, LANES - N_EXPERTS - N_GROUPS), F32)
    expert_ids = jnp.arange(N_EXPERTS, dtype=jnp.int32)
    block_start = jnp.arange(n_blocks, dtype=jnp.int32) * BM
    fill_words = (dump_base + (jnp.arange(n_slots, dtype=jnp.int32) & (2 * BM - 1))) << ROW_BITS

    h2_all = jnp.zeros((t_all * SUB, LANES), F32)
    route_all = jnp.zeros((t_all, LANES), F32)

    pool_p, pool_s, v_s = [], [], []
    for l in range(DEPTH):
        lw = {
            "w_in": w_in[l].astype(BF16),
            "w_out": w_out[l].astype(BF16),
            "pool_w": pool_w[l].astype(BF16),
            "pool_scale": pool_scale[l].reshape(1, D_POOL),
            "sgu_g": sgu_norm_g[l].reshape(1, D_SGU),
            "sgu_w": sgu_w[l],
            "sgu_b_t": sgu_b[l].T,
            "sgu_w0": jnp.repeat(sgu_w[l, :, 0, 0], SGU_HEAD_DIM).reshape(1, D_SGU),
            "sgu_b0": jnp.repeat(sgu_b[l, :, 0], SGU_HEAD_DIM).reshape(1, D_SGU),
            "ln1_g": ln1_g[l].reshape(1, D_MODEL),
            "ln1_b": ln1_b[l].reshape(1, D_MODEL),
            "w_r": _split_bf16(jnp.concatenate([router_e_w[l], router_g_w[l], lane_pad], axis=1)),
            "b_r": jnp.concatenate([router_e_b[l], router_g_b[l],
                                    jnp.zeros((LANES - N_EXPERTS - N_GROUPS,), F32)]).reshape(1, LANES),
            "ltri_p": ltri_p,
            "ltri_s": ltri_s,
        }
        mod_p = mod[l, :n_seq].reshape(n_seq, 6, D_MODEL)
        mod_s = mod[l, n_seq:]

        x1_all, h2_all, route_all, nbuf_p, cnt_p = _mix_prompt_call(
            x_all, h2_all, route_all, mod_p, lw, n_seq, seq_len)
        hist = jnp.transpose(state_pool[l], (1, 0, 2))
        x1_all, h2_all, route_all, nbuf_s, vn_s, cnt = _mix_sample_call(
            x1_all, h2_all, route_all, mod_s, hist, lw, cnt_p, n_dec, t_p)
        pool_p.append(nbuf_p[:, HIST - POOL_BUF:])
        pool_s.append(jnp.transpose(nbuf_s, (1, 0, 2)))
        v_s.append(vn_s.reshape(n_dec, 1, D_SGU))

        counts = cnt[0, :N_EXPERTS].astype(jnp.int32)
        padded = ((counts + BM - 1) // BM) * BM
        pends = jnp.cumsum(padded)
        pstarts = pends - padded
        e_sel = route_all[:, 0:2].astype(jnp.int32)
        rank = route_all[:, 4:6].astype(jnp.int32)
        dest = rank + jnp.sum(jnp.where(e_sel[..., None] == expert_ids, pstarts, 0), axis=-1)
        blk_valid = block_start < pends[-1]
        n_used = (pends[-1:] // BM).astype(jnp.int32)
        blk_e = jnp.sum((pends[None, :] <= block_start[:, None]).astype(jnp.int32), axis=1)
        blk_e = jnp.minimum(blk_e, N_EXPERTS - 1)
        last_e = jnp.max(jnp.where(blk_valid, blk_e, 0))
        blk_e = jnp.where(blk_valid, blk_e, last_e).astype(jnp.int32)
        blk_first = (blk_valid & jnp.any(pstarts[None, :] == block_start[:, None], axis=1)
                     ).astype(jnp.int32)

        xs, slots = _dispatch_call(pends, dest, h2_all, fill_words, n_blocks, n_assign // BM,
                                   choice_stride)
        y2 = _expert_call(l, blk_e, blk_first, n_used, slots.reshape(n_blocks, 1, BM), xs,
                          exp_w_gate, exp_w_up, exp_w_down, t_all, choice_stride)

        ln2g = ln2_g[l].reshape(1, D_MODEL)
        ln2b = ln2_b[l].reshape(1, D_MODEL)
        tiles_per_seq = seq_len // TM_ROW
        g2_p_spec = pl.BlockSpec((None, 1, D_MODEL), lambda i: (i // tiles_per_seq, 0, 0))
        last = l == DEPTH - 1
        out_p = _combine_call(x1_all, route_all, y2, mod_p[:, 5:6, :], g2_p_spec, ln2g, ln2b,
                              TM_ROW, 0, t_p, choice_stride, in_place=not last)
        out_s = _combine_call(x1_all if last else out_p, route_all, y2, mod_s[:, 5 * D_MODEL:],
                              _full((n_dec, D_MODEL)), ln2g, ln2b,
                              n_dec, t_p, n_dec, choice_stride, in_place=not last)
        x_all = out_s

    return (out_p.reshape(n_seq, seq_len, D_MODEL),
            out_s.reshape(n_dec, 1, D_MODEL),
            jnp.stack(pool_p, axis=0),
            jnp.stack(pool_s, axis=0),
            jnp.stack(v_s, axis=0))
```

```python
import functools

import jax
import jax.numpy as jnp
from jax import lax
from jax.experimental import pallas as pl
from jax.experimental.pallas import tpu as pltpu

D_MODEL = 1024
DEPTH = 4
PAST_LEN = 16384
D_POOL = 512
N_POOL_GROUPS = 4
POOL_GROUP_DIM = 128
POOL_WINDOWS = (2, 4, 8, 16)
POOL_BUF = 15
HIST = 16
D_SGU = 512
SGU_HEADS = 4
SGU_HEAD_DIM = 128
CHUNK = 128
N_GROUPS = 4
EXPERTS_PER_GROUP = 8
N_EXPERTS = 32
D_EXPERT = 512
DEEPNORM_ALPHA = (2.0 * DEPTH) ** 0.25
LN_EPS = 1e-5

LANES = 128
SUB = 8
assert D_MODEL == SUB * LANES
ROUTE_G_LANE = N_EXPERTS
SECOND_LANE = 64

TM_MIX = 512
TM_ROW = 256
TM_DISPATCH = 384
BM = 256
ROW_BITS = 16

F32 = jnp.float32
BF16 = jnp.bfloat16
VMEM_LIMIT = 48 * 1024 * 1024


def _layer_norm(r, g, b):
    mu = jnp.mean(r, axis=-1, keepdims=True)
    rc = r - mu
    var = jnp.mean(rc * rc, axis=-1, keepdims=True)
    return rc * lax.rsqrt(var + LN_EPS) * g + b


def _head_norm(vh, g):
    mu = jnp.mean(vh, axis=-1, keepdims=True)
    vc = vh - mu
    var = jnp.mean(vc * vc, axis=-1, keepdims=True)
    return vc * lax.rsqrt(var + LN_EPS) * g


def _roll_half(row):
    return pltpu.roll(jnp.broadcast_to(row, (8, LANES)), SECOND_LANE, 1)[0:1]


def _store_row_tiles(ref, val):
    rows = val.shape[0]
    for c in range(SUB):
        ref[pl.ds(c, rows, stride=SUB), :] = val[:, c * LANES:(c + 1) * LANES]


def _load_row_tiles(ref, rows):
    return jnp.concatenate(
        [ref[pl.ds(c, rows, stride=SUB), :] for c in range(SUB)], axis=1)


def _route(logits, ltri_ref, run_ref):
    tm = logits.shape[0]
    lane = lax.broadcasted_iota(jnp.int32, (tm, LANES), 1)
    lanef = lane.astype(F32)
    neg = -jnp.inf
    is_g = (lane >= ROUTE_G_LANE) & (lane < ROUTE_G_LANE + N_GROUPS)
    glm = jnp.where(is_g, logits, neg)
    gmax = jnp.max(glm, axis=1, keepdims=True)
    g_idx = jnp.min(jnp.where(glm == gmax, lanef - ROUTE_G_LANE, 1e4), axis=1, keepdims=True)
    p_g = 1.0 / jnp.sum(jnp.exp(glm - gmax), axis=1, keepdims=True)

    in_grp = (lane < N_EXPERTS) & ((lane >> 3).astype(F32) == g_idx)
    elm = jnp.where(in_grp, logits, neg)
    m1 = jnp.max(elm, axis=1, keepdims=True)
    i1 = jnp.min(jnp.where(elm == m1, lanef, 1e4), axis=1, keepdims=True)
    elm2 = jnp.where(lanef == i1, neg, elm)
    m2 = jnp.max(elm2, axis=1, keepdims=True)
    i2 = jnp.min(jnp.where(elm2 == m2, lanef, 1e4), axis=1, keepdims=True)
    e21 = jnp.exp(m2 - m1)
    den = 1.0 + e21
    w0 = (1.0 / den) * p_g
    w1 = (e21 / den) * p_g

    hit0 = lanef == i1
    hit1 = lanef == i2 + SECOND_LANE
    oh = jnp.where(hit0 | hit1, 1.0, 0.0)
    before = jnp.dot(ltri_ref[...], oh.astype(BF16), preferred_element_type=F32)
    tot = jnp.sum(oh, axis=0, keepdims=True)
    lane1 = lax.broadcasted_iota(jnp.int32, (1, LANES), 1)
    tot0 = jnp.where(lane1 < N_EXPERTS, tot, 0.0)
    tot1 = jnp.where(lane1 >= SECOND_LANE, tot, 0.0)
    run = run_ref[...]
    base = run + _roll_half(run + tot0)
    val = before + base
    rank0 = jnp.sum(jnp.where(hit0, val, 0.0), axis=1, keepdims=True)
    rank1 = jnp.sum(jnp.where(hit1, val, 0.0), axis=1, keepdims=True)
    run_ref[...] = run + tot0 + _roll_half(tot1)

    out = jnp.where(lane == 0, i1, 0.0)
    out = jnp.where(lane == 1, i2, out)
    out = jnp.where(lane == 2, w0, out)
    out = jnp.where(lane == 3, w1, out)
    out = jnp.where(lane == 4, rank0, out)
    out = jnp.where(lane == 5, rank1, out)
    return out


def _finish_mix(x, mix, g1, sh2, sc2, ln1g_ref, ln1b_ref, wr_ref, br_ref, ltri_ref, run_ref,
                x1_ref, h2_ref, route_ref):
    r = DEEPNORM_ALPHA * x + (1.0 + g1) * mix
    x1 = _layer_norm(r, ln1g_ref[...], ln1b_ref[...])
    x1_ref[...] = x1
    h2 = x1 * (1.0 + sc2) + sh2
    _store_row_tiles(h2_ref, h2)
    hi = h2.astype(BF16)
    lo = (h2 - hi.astype(F32)).astype(BF16)
    both = jnp.dot(hi, wr_ref[...], preferred_element_type=F32)
    logits = (both[:, :LANES] + both[:, LANES:]
              + jnp.dot(lo, wr_ref[:, :LANES], preferred_element_type=F32) + br_ref[...])
    route_ref[...] = _route(logits, ltri_ref, run_ref)


def _masked_sgu_w(sguw_ref, hd):
    t = lax.broadcasted_iota(jnp.int32, (CHUNK, CHUNK), 0)
    s = lax.broadcasted_iota(jnp.int32, (CHUNK, CHUNK), 1)
    return jnp.where(s <= t, sguw_ref[hd], 0.0).astype(BF16)


def _full(shape):
    nd = len(shape)
    return pl.BlockSpec(shape, lambda *_: (0,) * nd)


_ANY = pl.BlockSpec(memory_space=pl.ANY)


def _mod_kernel(c_ref, w_ref, b_ref, o_ref):
    s = jax.nn.silu(c_ref[...])
    o_ref[...] = jnp.dot(s.astype(BF16), w_ref[...].astype(BF16),
                         preferred_element_type=F32) + b_ref[...]


def _mod_call(c_all, w_ada, b_ada):
    n = c_all.shape[0]
    tn = 1024
    return pl.pallas_call(
        _mod_kernel,
        grid=(DEPTH, 6 * D_MODEL // tn),
        in_specs=[
            pl.BlockSpec((n, D_MODEL), lambda l, j: (0, 0)),
            pl.BlockSpec((None, D_MODEL, tn), lambda l, j: (l, 0, j)),
            pl.BlockSpec((None, 1, tn), lambda l, j: (l, 0, j)),
        ],
        out_specs=pl.BlockSpec((None, n, tn), lambda l, j: (l, 0, j)),
        out_shape=jax.ShapeDtypeStruct((DEPTH, n, 6 * D_MODEL), F32),
        name="adaln_mod",
        compiler_params=pltpu.CompilerParams(
            dimension_semantics=("arbitrary", "arbitrary")),
    )(c_all, w_ada, b_ada.reshape(DEPTH, 1, 6 * D_MODEL))


def _mix_prompt_kernel(x_ref, mod_ref, win_ref, wout_ref, poolw_ref, pscale_ref, sgug_ref,
                       sguw_ref, sgub_ref, ln1g_ref, ln1b_ref, wr_ref, br_ref, ltri_ref,
                       h2_in_ref, route_in_ref,
                       x1_ref, h2_ref, route_ref, nbuf_ref, cnt_ref,
                       pe_ref, mixin_ref, run_ref, *, tm, tiles_per_seq):
    del h2_in_ref, route_in_ref
    b = pl.program_id(0)
    j = pl.program_id(1)

    @pl.when((b == 0) & (j == 0))
    def _():
        run_ref[...] = jnp.zeros_like(run_ref)

    @pl.when(j == 0)
    def _():
        pe_ref[0:HIST, :] = jnp.zeros((HIST, D_POOL), F32)

    x = x_ref[...]
    mod = mod_ref[...]
    sh1, sc1, g1, sh2, sc2 = (mod[i:i + 1] for i in range(5))
    h = x * (1.0 + sc1) + sh1
    proj = jnp.dot(h.astype(BF16), win_ref[...], preferred_element_type=F32)
    p = proj[:, :D_POOL]
    u = proj[:, D_POOL:D_POOL + D_SGU]
    v = proj[:, D_POOL + D_SGU:]

    pe_ref[HIST:HIST + tm, :] = p
    pos = j * tm + lax.broadcasted_iota(jnp.int32, (tm, 1), 0)
    for g, w in enumerate(POOL_WINDOWS):
        lo, hi = g * POOL_GROUP_DIM, (g + 1) * POOL_GROUP_DIM
        pg = p[:, lo:hi]
        s = pg
        for k in range(1, w):
            s = s + pe_ref[pl.ds(HIST - k, tm), lo:hi]
        cnt = jnp.minimum(pos + 1, w).astype(F32)
        d = s / cnt - pg
        a = jnp.dot(d.astype(BF16), poolw_ref[g], preferred_element_type=F32)
        mixin_ref[:, lo:hi] = (a * pscale_ref[:, lo:hi]).astype(BF16)

    @pl.when(j == tiles_per_seq - 1)
    def _():
        nbuf_ref[...] = pe_ref[tm:tm + HIST, :]

    pe_ref[0:HIST, :] = pe_ref[tm:tm + HIST, :]

    for hd in range(SGU_HEADS):
        lo, hi = hd * SGU_HEAD_DIM, (hd + 1) * SGU_HEAD_DIM
        vn = _head_norm(v[:, lo:hi], sgug_ref[:, lo:hi]).astype(BF16)
        ws = _masked_sgu_w(sguw_ref, hd)
        bcol = sgub_ref[:, hd:hd + 1]
        for c in range(tm // CHUNK):
            r0, r1 = c * CHUNK, (c + 1) * CHUNK
            z = jnp.dot(ws, vn[r0:r1], preferred_element_type=F32) + bcol
            mixin_ref[r0:r1, D_POOL + lo:D_POOL + hi] = (u[r0:r1, lo:hi] * z).astype(BF16)

    mix = jnp.dot(mixin_ref[...], wout_ref[...], preferred_element_type=F32)
    _finish_mix(x, mix, g1, sh2, sc2, ln1g_ref, ln1b_ref, wr_ref, br_ref, ltri_ref, run_ref,
                x1_ref, h2_ref, route_ref)
    cnt_ref[...] = run_ref[...]


def _mix_prompt_call(x_all, h2_buf, route_buf, mod_p, lw, n_seq, seq_len):
    tm = TM_MIX
    tps = seq_len // tm
    t_all = x_all.shape[0]
    row = lambda b, j: (b * tps + j, 0)
    return pl.pallas_call(
        functools.partial(_mix_prompt_kernel, tm=tm, tiles_per_seq=tps),
        grid=(n_seq, tps),
        in_specs=[
            pl.BlockSpec((tm, D_MODEL), row),
            pl.BlockSpec((None, 6, D_MODEL), lambda b, j: (b, 0, 0)),
            _full((D_MODEL, D_POOL + 2 * D_SGU)),
            _full((D_POOL + D_SGU, D_MODEL)),
            _full((N_POOL_GROUPS, POOL_GROUP_DIM, POOL_GROUP_DIM)),
            _full((1, D_POOL)),
            _full((1, D_SGU)),
            _full((SGU_HEADS, CHUNK, CHUNK)),
            _full((CHUNK, SGU_HEADS)),
            _full((1, D_MODEL)),
            _full((1, D_MODEL)),
            _full((D_MODEL, 2 * LANES)),
            _full((1, LANES)),
            _full((tm, tm)),
            _ANY, _ANY,
        ],
        out_specs=[
            pl.BlockSpec((tm, D_MODEL), row),
            pl.BlockSpec((tm * SUB, LANES), row),
            pl.BlockSpec((tm, LANES), row),
            pl.BlockSpec((None, HIST, D_POOL), lambda b, j: (b, 0, 0)),
            pl.BlockSpec((1, LANES), lambda b, j: (0, 0)),
        ],
        out_shape=[
            jax.ShapeDtypeStruct((t_all, D_MODEL), F32),
            jax.ShapeDtypeStruct((t_all * SUB, LANES), F32),
            jax.ShapeDtypeStruct((t_all, LANES), F32),
            jax.ShapeDtypeStruct((n_seq, HIST, D_POOL), F32),
            jax.ShapeDtypeStruct((1, LANES), F32),
        ],
        scratch_shapes=[
            pltpu.VMEM((HIST + tm, D_POOL), F32),
            pltpu.VMEM((tm, D_POOL + D_SGU), BF16),
            pltpu.VMEM((1, LANES), F32),
        ],
        input_output_aliases={0: 0, 14: 1, 15: 2},
        name="mix_prompt",
        compiler_params=pltpu.CompilerParams(
            dimension_semantics=("arbitrary", "arbitrary"),
            vmem_limit_bytes=VMEM_LIMIT),
    )(x_all, mod_p, lw["w_in"], lw["w_out"], lw["pool_w"], lw["pool_scale"], lw["sgu_g"],
      lw["sgu_w"], lw["sgu_b_t"], lw["ln1_g"], lw["ln1_b"], lw["w_r"], lw["b_r"], lw["ltri_p"],
      h2_buf, route_buf)


def _mix_sample_kernel(x_ref, mod_ref, hist_ref, win_ref, wout_ref, poolw_ref, pscale_ref,
                       sgug_ref, sguw0_ref, sgub0_ref, ln1g_ref, ln1b_ref, wr_ref, br_ref,
                       ltri_ref, cnt_in_ref, h2_in_ref, route_in_ref,
                       x1_ref, h2_ref, route_ref, nbuf_ref, vn_ref, cnt_ref,
                       mixin_ref, run_ref):
    del h2_in_ref, route_in_ref
    run_ref[...] = cnt_in_ref[...]
    x = x_ref[...]
    sh1, sc1, g1, sh2, sc2 = (mod_ref[:, i * D_MODEL:(i + 1) * D_MODEL] for i in range(5))
    h = x * (1.0 + sc1) + sh1
    proj = jnp.dot(h.astype(BF16), win_ref[...], preferred_element_type=F32)
    p = proj[:, :D_POOL]
    u = proj[:, D_POOL:D_POOL + D_SGU]
    v = proj[:, D_POOL + D_SGU:]

    for g, w in enumerate(POOL_WINDOWS):
        lo, hi = g * POOL_GROUP_DIM, (g + 1) * POOL_GROUP_DIM
        pg = p[:, lo:hi]
        s = pg
        for k in range(1, w):
            s = s + hist_ref[POOL_BUF - k, :, lo:hi]
        cnt = float(min(PAST_LEN + 1, w))
        d = s / cnt - pg
        a = jnp.dot(d.astype(BF16), poolw_ref[g], preferred_element_type=F32)
        mixin_ref[:, lo:hi] = (a * pscale_ref[:, lo:hi]).astype(BF16)

    for k in range(POOL_BUF - 1):
        nbuf_ref[k] = hist_ref[k + 1]
    nbuf_ref[POOL_BUF - 1] = p

    for hd in range(SGU_HEADS):
        lo, hi = hd * SGU_HEAD_DIM, (hd + 1) * SGU_HEAD_DIM
        vn = _head_norm(v[:, lo:hi], sgug_ref[:, lo:hi])
        vn_ref[:, lo:hi] = vn
        z = sguw0_ref[:, lo:hi].astype(BF16).astype(F32) * vn.astype(BF16).astype(F32) \
            + sgub0_ref[:, lo:hi]
        mixin_ref[:, D_POOL + lo:D_POOL + hi] = (u[:, lo:hi] * z).astype(BF16)

    mix = jnp.dot(mixin_ref[...], wout_ref[...], preferred_element_type=F32)
    _finish_mix(x, mix, g1, sh2, sc2, ln1g_ref, ln1b_ref, wr_ref, br_ref, ltri_ref, run_ref,
                x1_ref, h2_ref, route_ref)
    cnt_ref[...] = run_ref[...]


def _mix_sample_call(x1_all, h2_all, route_all, mod_s, hist, lw, cnt_in, n_dec, row0):
    blk = row0 // n_dec
    rows = lambda r, w: pl.BlockSpec((r, w), lambda i: (blk, 0))
    return pl.pallas_call(
        _mix_sample_kernel,
        grid=(1,),
        in_specs=[
            rows(n_dec, D_MODEL),
            _full((n_dec, 6 * D_MODEL)),
            _full((POOL_BUF, n_dec, D_POOL)),
            _full((D_MODEL, D_POOL + 2 * D_SGU)),
            _full((D_POOL + D_SGU, D_MODEL)),
            _full((N_POOL_GROUPS, POOL_GROUP_DIM, POOL_GROUP_DIM)),
            _full((1, D_POOL)),
            _full((1, D_SGU)),
            _full((1, D_SGU)),
            _full((1, D_SGU)),
            _full((1, D_MODEL)),
            _full((1, D_MODEL)),
            _full((D_MODEL, 2 * LANES)),
            _full((1, LANES)),
            _full((n_dec, n_dec)),
            _full((1, LANES)),
            _ANY, _ANY,
        ],
        out_specs=[
            rows(n_dec, D_MODEL),
            rows(n_dec * SUB, LANES),
            rows(n_dec, LANES),
            _full((POOL_BUF, n_dec, D_POOL)),
            _full((n_dec, D_SGU)),
            _full((1, LANES)),
        ],
        out_shape=[
            jax.ShapeDtypeStruct(x1_all.shape, F32),
            jax.ShapeDtypeStruct(h2_all.shape, F32),
            jax.ShapeDtypeStruct(route_all.shape, F32),
            jax.ShapeDtypeStruct((POOL_BUF, n_dec, D_POOL), F32),
            jax.ShapeDtypeStruct((n_dec, D_SGU), F32),
            jax.ShapeDtypeStruct((1, LANES), F32),
        ],
        scratch_shapes=[
            pltpu.VMEM((n_dec, D_POOL + D_SGU), BF16),
            pltpu.VMEM((1, LANES), F32),
        ],
        input_output_aliases={0: 0, 16: 1, 17: 2},
        name="mix_sample",
        compiler_params=pltpu.CompilerParams(
            dimension_semantics=("arbitrary",),
            vmem_limit_bytes=VMEM_LIMIT),
    )(x1_all, mod_s, hist, lw["w_in"], lw["w_out"], lw["pool_w"], lw["pool_scale"], lw["sgu_g"],
      lw["sgu_w0"], lw["sgu_b0"], lw["ln1_g"], lw["ln1_b"], lw["w_r"], lw["b_r"], lw["ltri_s"],
      cnt_in, h2_all, route_all)


def _tile_copy(src_ref, src_row, dst_ref, dst_row, sem):
    def tile(ref, row):
        start = row * SUB if isinstance(row, int) else pl.multiple_of(row * SUB, SUB)
        return ref.at[pl.ds(start, SUB), :]

    return pltpu.make_async_copy(tile(src_ref, src_row), tile(dst_ref, dst_row), sem)


def _dispatch_kernel(pends_ref, dest_ref, h2_ref, fill_ref, xs_ref, slots_ref, zero_ref, sem,
                     *, tm, n_blocks, min_used, choice_stride):
    i = pl.program_id(0)

    @pl.when(i == 0)
    def _():
        fill = pltpu.make_async_copy(fill_ref, slots_ref, sem)
        fill.start()
        fill.wait()
        zero_ref[...] = jnp.zeros_like(zero_ref)
        n_used = lax.shift_right_logical(pends_ref[N_EXPERTS - 1], BM.bit_length() - 1)

        def block_copy(b):
            first = pl.multiple_of(b * (BM * SUB), BM * SUB)
            return pltpu.make_async_copy(zero_ref, xs_ref.at[pl.ds(first, BM * SUB), :], sem)

        def last_block(e):
            return lax.shift_right_logical(pends_ref[e], BM.bit_length() - 1) - 1

        for act in ("start", "wait"):
            for e in range(N_EXPERTS):
                prev_end = pends_ref[e - 1] if e else 0
                pl.when(pends_ref[e] > prev_end)(
                    lambda e=e, act=act: getattr(block_copy(last_block(e)), act)())
            for b in range(min_used, n_blocks):
                pl.when(b >= n_used)(lambda b=b, act=act: getattr(block_copy(b), act)())

    tile_word = (i * tm) << ROW_BITS
    for r in range(tm):
        for k in range(2):
            slot = dest_ref[0, 2 * r + k]
            _tile_copy(h2_ref, r, xs_ref, slot, sem).start(priority=k)
            slots_ref[slot] = tile_word + ((k * choice_stride + r) << ROW_BITS)
    for r in range(tm):
        for k in range(2):
            _tile_copy(h2_ref, r, xs_ref, 0, sem).wait()


def _dispatch_call(pends, dest, h2_all, fill_words, n_blocks, min_used, choice_stride):
    tm = TM_DISPATCH
    n_tiles = dest.shape[0] // tm
    grid_spec = pltpu.PrefetchScalarGridSpec(
        num_scalar_prefetch=1,
        grid=(n_tiles,),
        in_specs=[
            pl.BlockSpec((None, 1, 2 * tm), lambda i, pe: (i, 0, 0), memory_space=pltpu.SMEM),
            pl.BlockSpec((tm * SUB, LANES), lambda i, pe: (i, 0)),
            _ANY,
        ],
        out_specs=[_ANY, pl.BlockSpec(memory_space=pltpu.SMEM)],
        scratch_shapes=[pltpu.VMEM((BM * SUB, LANES), F32), pltpu.SemaphoreType.DMA],
    )
    return pl.pallas_call(
        functools.partial(_dispatch_kernel, tm=tm, n_blocks=n_blocks, min_used=min_used,
                          choice_stride=choice_stride),
        grid_spec=grid_spec,
        out_shape=[jax.ShapeDtypeStruct((n_blocks * BM * SUB, LANES), F32),
                   jax.ShapeDtypeStruct(fill_words.shape, jnp.int32)],
        name="dispatch_rows",
        compiler_params=pltpu.CompilerParams(dimension_semantics=("arbitrary",)),
    )(pends, dest.reshape(n_tiles, 1, 2 * tm), h2_all, fill_words)


SCATTER_GROUPS = 4


def _expert_kernel(be_ref, bfirst_ref, nused_ref, slot_prev_ref, slot_ref, xs_ref,
                   wg_ref, wu_ref, wd_ref,
                   y2_ref, ybuf_ref, wgb_ref, wub_ref, wdb_ref, ssem,
                   *, dump_base, unowned_rows):
    del be_ref
    i = pl.program_id(0)
    n_used = nused_ref[0]

    def start_scatter(words_ref, buf, rows=range(BM)):
        for r in rows:
            row = lax.shift_right_logical(words_ref[0, r], ROW_BITS)
            _tile_copy(ybuf_ref.at[buf], r, y2_ref, row, ssem.at[buf]).start(priority=r % 2)

    def wait_scatter(buf):
        for r in range(BM):
            _tile_copy(ybuf_ref.at[buf], r, y2_ref, 0, ssem.at[buf]).wait()

    @pl.when(i == 0)
    def _():
        ybuf_ref[...] = jnp.zeros_like(ybuf_ref)
        unowned = [row for lo, hi in unowned_rows for row in range(lo, hi)]
        assert len(unowned) <= BM
        for r, row in enumerate(unowned):
            _tile_copy(ybuf_ref.at[1], r, y2_ref, row, ssem.at[1]).start(priority=r % 2)
        for r, row in enumerate(unowned):
            _tile_copy(ybuf_ref.at[1], r, y2_ref, row, ssem.at[1]).wait()
        for buf in range(2):
            for r in range(BM):
                _tile_copy(ybuf_ref.at[buf], r, y2_ref, dump_base + buf * BM + r,
                           ssem.at[buf]).start(priority=r % 2)

    @pl.when(bfirst_ref[i] == 1)
    def _():
        wgb_ref[...] = wg_ref[...].astype(BF16)
        wub_ref[...] = wu_ref[...].astype(BF16)
        wdb_ref[...] = wd_ref[...].astype(BF16)

    def step(buf, scatter_prev):
        per_group = BM // SCATTER_GROUPS
        groups = iter(range(g * per_group, (g + 1) * per_group) for g in range(SCATTER_GROUPS))

        def scatter_piece():
            if scatter_prev:
                start_scatter(slot_prev_ref, 1 - buf, next(groups))

        x = _load_row_tiles(xs_ref, BM).astype(BF16)
        half = D_EXPERT // 2
        acts = []
        for lo in (0, half):
            g = jnp.dot(x, wgb_ref[:, lo:lo + half], preferred_element_type=F32)
            u = jnp.dot(x, wub_ref[:, lo:lo + half], preferred_element_type=F32)
            acts.append((jax.nn.silu(g) * u).astype(BF16))
            scatter_piece()
        a = jnp.concatenate(acts, axis=1)
        half = D_MODEL // 2
        ys = []
        for lo in (0, half):
            ys.append(jnp.dot(a, wdb_ref[:, lo:lo + half], preferred_element_type=F32))
            scatter_piece()
        y = jnp.concatenate(ys, axis=1)
        wait_scatter(buf)
        _store_row_tiles(ybuf_ref.at[buf], y)

        @pl.when(i == n_used - 1)
        def _():
            start_scatter(slot_ref, buf)
            wait_scatter(1 - buf)
            wait_scatter(buf)

    pl.when(i == 0)(functools.partial(step, 0, False))
    for buf in range(2):
        pl.when((i > 0) & (i < n_used) & (i % 2 == buf))(functools.partial(step, buf, True))


def _expert_call(layer, blk_e, blk_first, n_used, slots, xs, w_gate, w_up, w_down,
                 t_all, choice_stride):
    n_blocks = blk_e.shape[0]
    dump_base = 2 * choice_stride
    n_y_rows = dump_base + 2 * BM
    unowned_rows = ((t_all, choice_stride), (choice_stride + t_all, 2 * choice_stride))
    wmap = lambda i, be, bf, nu: (layer, be[i], 0, 0)
    grid_spec = pltpu.PrefetchScalarGridSpec(
        num_scalar_prefetch=3,
        grid=(n_blocks,),
        in_specs=[
            pl.BlockSpec((None, 1, BM),
                         lambda i, be, bf, nu: (jnp.clip(i - 1, 0, nu[0] - 1), 0, 0),
                         memory_space=pltpu.SMEM),
            pl.BlockSpec((None, 1, BM), lambda i, be, bf, nu: (jnp.minimum(i, nu[0] - 1), 0, 0),
                         memory_space=pltpu.SMEM),
            pl.BlockSpec((BM * SUB, LANES), lambda i, be, bf, nu: (jnp.minimum(i, nu[0] - 1), 0)),
            pl.BlockSpec((None, None, D_MODEL, D_EXPERT), wmap),
            pl.BlockSpec((None, None, D_MODEL, D_EXPERT), wmap),
            pl.BlockSpec((None, None, D_EXPERT, D_MODEL), wmap),
        ],
        out_specs=_ANY,
        scratch_shapes=[
            pltpu.VMEM((2, BM * SUB, LANES), F32),
            pltpu.VMEM((D_MODEL, D_EXPERT), BF16),
            pltpu.VMEM((D_MODEL, D_EXPERT), BF16),
            pltpu.VMEM((D_EXPERT, D_MODEL), BF16),
            pltpu.SemaphoreType.DMA((2,)),
        ],
    )
    return pl.pallas_call(
        functools.partial(_expert_kernel, dump_base=dump_base, unowned_rows=unowned_rows),
        grid_spec=grid_spec,
        out_shape=jax.ShapeDtypeStruct((n_y_rows * SUB, LANES), F32),
        name="expert_mlp",
        compiler_params=pltpu.CompilerParams(
            dimension_semantics=("arbitrary",),
            vmem_limit_bytes=VMEM_LIMIT),
    )(blk_e, blk_first, n_used, slots, slots, xs, w_gate, w_up, w_down)


def _combine_kernel(x1_ref, route_ref, y0_ref, y1_ref, g2_ref, ln2g_ref, ln2b_ref, out_ref, *, tm):
    route = route_ref[...]
    f = route[:, 2:3] * _load_row_tiles(y0_ref, tm) + route[:, 3:4] * _load_row_tiles(y1_ref, tm)
    r = DEEPNORM_ALPHA * x1_ref[...] + (1.0 + g2_ref[...]) * f
    out_ref[...] = _layer_norm(r, ln2g_ref[...], ln2b_ref[...])


def _combine_call(x1_all, route_all, y2, g2, g2_spec, ln2_g, ln2_b, tm, row0, n_rows,
                  choice_stride, in_place):
    blk0 = row0 // tm
    blk1 = (choice_stride + row0) // tm
    assert row0 % tm == 0 and n_rows % tm == 0 and choice_stride % tm == 0
    rows = lambda i: (blk0 + i, 0)
    if in_place:
        out_spec = pl.BlockSpec((tm, D_MODEL), rows)
        out_shape = jax.ShapeDtypeStruct(x1_all.shape, F32)
        aliases = {0: 0}
    else:
        out_spec = pl.BlockSpec((tm, D_MODEL), lambda i: (i, 0))
        out_shape = jax.ShapeDtypeStruct((n_rows, D_MODEL), F32)
        aliases = {}
    return pl.pallas_call(
        functools.partial(_combine_kernel, tm=tm),
        grid=(n_rows // tm,),
        in_specs=[
            pl.BlockSpec((tm, D_MODEL), rows),
            pl.BlockSpec((tm, LANES), rows),
            pl.BlockSpec((tm * SUB, LANES), rows),
            pl.BlockSpec((tm * SUB, LANES), lambda i: (blk1 + i, 0)),
            g2_spec,
            _full((1, D_MODEL)),
            _full((1, D_MODEL)),
        ],
        out_specs=out_spec,
        out_shape=out_shape,
        input_output_aliases=aliases,
        name="combine_ln2",
        compiler_params=pltpu.CompilerParams(
            dimension_semantics=("arbitrary",),
            vmem_limit_bytes=VMEM_LIMIT),
    )(x1_all, route_all, y2, y2, g2, ln2_g, ln2_b)


def _split_bf16(w):
    hi = w.astype(BF16)
    lo = (w - hi.astype(F32)).astype(BF16)
    return jnp.concatenate([hi, lo], axis=1)


def _strict_lower(n):
    r = lax.broadcasted_iota(jnp.int32, (n, n), 0)
    c = lax.broadcasted_iota(jnp.int32, (n, n), 1)
    return (c < r).astype(BF16)


def kernel(x_prompt, x_sample, state_pool, c_prompt, c_sample, w_ada, b_ada, w_in, pool_w, pool_scale, sgu_norm_g, sgu_w, sgu_b, w_out, ln1_g, ln1_b, router_g_w, router_g_b, router_e_w, router_e_b, exp_w_gate, exp_w_up, exp_w_down, ln2_g, ln2_b):
    n_seq, seq_len, _ = x_prompt.shape
    n_dec = x_sample.shape[0]
    t_p = n_seq * seq_len
    t_all = t_p + n_dec
    assert t_p % n_dec == 0 and seq_len % TM_MIX == 0 and seq_len % TM_ROW == 0
    assert t_all % TM_DISPATCH == 0
    n_assign = 2 * t_all
    n_blocks = -(-n_assign // BM) + N_EXPERTS
    n_slots = n_blocks * BM
    choice_stride = -(-t_all // TM_ROW) * TM_ROW
    dump_base = 2 * choice_stride
    n_y_rows = dump_base + 2 * BM
    assert n_y_rows <= (1 << (32 - ROW_BITS)) and t_all <= (1 << ROW_BITS)

    x_all = jnp.concatenate([x_prompt.reshape(t_p, D_MODEL), x_sample.reshape(n_dec, D_MODEL)])
    mod = _mod_call(jnp.concatenate([c_prompt, c_sample], axis=0), w_ada, b_ada)
    ltri_p = _strict_lower(TM_MIX)
    ltri_s = _strict_lower(n_dec)
    lane_pad = jnp.zeros((D_MODEL, LANES - N_EXPERTS - N_GROUPS), F32)
    expert_ids = jnp.arange(N_EXPERTS, dtype=jnp.int32)
    block_start = jnp.arange(n_blocks, dtype=jnp.int32) * BM
    fill_words = (dump_base + (jnp.arange(n_slots, dtype=jnp.int32) & (2 * BM - 1))) << ROW_BITS

    h2_all = jnp.zeros((t_all * SUB, LANES), F32)
    route_all = jnp.zeros((t_all, LANES), F32)

    pool_p, pool_s, v_s = [], [], []
    for l in range(DEPTH):
        lw = {
            "w_in": w_in[l].astype(BF16),
            "w_out": w_out[l].astype(BF16),
            "pool_w": pool_w[l].astype(BF16),
            "pool_scale": pool_scale[l].reshape(1, D_POOL),
            "sgu_g": sgu_norm_g[l].reshape(1, D_SGU),
            "sgu_w": sgu_w[l],
            "sgu_b_t": sgu_b[l].T,
            "sgu_w0": jnp.repeat(sgu_w[l, :, 0, 0], SGU_HEAD_DIM).reshape(1, D_SGU),
            "sgu_b0": jnp.repeat(sgu_b[l, :, 0], SGU_HEAD_DIM).reshape(1, D_SGU),
            "ln1_g": ln1_g[l].reshape(1, D_MODEL),
            "ln1_b": ln1_b[l].reshape(1, D_MODEL),
            "w_r": _split_bf16(jnp.concatenate([router_e_w[l], router_g_w[l], lane_pad], axis=1)),
            "b_r": jnp.concatenate([router_e_b[l], router_g_b[l],
                                    jnp.zeros((LANES - N_EXPERTS - N_GROUPS,), F32)]).reshape(1, LANES),
            "ltri_p": ltri_p,
            "ltri_s": ltri_s,
        }
        mod_p = mod[l, :n_seq].reshape(n_seq, 6, D_MODEL)
        mod_s = mod[l, n_seq:]

        x1_all, h2_all, route_all, nbuf_p, cnt_p = _mix_prompt_call(
            x_all, h2_all, route_all, mod_p, lw, n_seq, seq_len)
        hist = jnp.transpose(state_pool[l], (1, 0, 2))
        x1_all, h2_all, route_all, nbuf_s, vn_s, cnt = _mix_sample_call(
            x1_all, h2_all, route_all, mod_s, hist, lw, cnt_p, n_dec, t_p)
        pool_p.append(nbuf_p[:, HIST - POOL_BUF:])
        pool_s.append(jnp.transpose(nbuf_s, (1, 0, 2)))
        v_s.append(vn_s.reshape(n_dec, 1, D_SGU))

        counts = cnt[0, :N_EXPERTS].astype(jnp.int32)
        padded = ((counts + BM - 1) // BM) * BM
        pends = jnp.cumsum(padded)
        pstarts = pends - padded
        e_sel = route_all[:, 0:2].astype(jnp.int32)
        rank = route_all[:, 4:6].astype(jnp.int32)
        dest = rank + jnp.sum(jnp.where(e_sel[..., None] == expert_ids, pstarts, 0), axis=-1)
        blk_valid = block_start < pends[-1]
        n_used = (pends[-1:] // BM).astype(jnp.int32)
        blk_e = jnp.sum((pends[None, :] <= block_start[:, None]).astype(jnp.int32), axis=1)
        blk_e = jnp.minimum(blk_e, N_EXPERTS - 1)
        last_e = jnp.max(jnp.where(blk_valid, blk_e, 0))
        blk_e = jnp.where(blk_valid, blk_e, last_e).astype(jnp.int32)
        blk_first = (blk_valid & jnp.any(pstarts[None, :] == block_start[:, None], axis=1)
                     ).astype(jnp.int32)

        xs, slots = _dispatch_call(pends, dest, h2_all, fill_words, n_blocks, n_assign // BM,
                                   choice_stride)
        y2 = _expert_call(l, blk_e, blk_first, n_used, slots.reshape(n_blocks, 1, BM), xs,
                          exp_w_gate, exp_w_up, exp_w_down, t_all, choice_stride)

        ln2g = ln2_g[l].reshape(1, D_MODEL)
        ln2b = ln2_b[l].reshape(1, D_MODEL)
        tiles_per_seq = seq_len // TM_ROW
        g2_p_spec = pl.BlockSpec((None, 1, D_MODEL), lambda i: (i // tiles_per_seq, 0, 0))
        last = l == DEPTH - 1
        out_p = _combine_call(x1_all, route_all, y2, mod_p[:, 5:6, :], g2_p_spec, ln2g, ln2b,
                              TM_ROW, 0, t_p, choice_stride, in_place=not last)
        out_s = _combine_call(x1_all if last else out_p, route_all, y2, mod_s[:, 5 * D_MODEL:],
                              _full((n_dec, D_MODEL)), ln2g, ln2b,
                              n_dec, t_p, n_dec, choice_stride, in_place=not last)
        x_all = out_s

    return (out_p.reshape(n_seq, seq_len, D_MODEL),
            out_s.reshape(n_dec, 1, D_MODEL),
            jnp.stack(pool_p, axis=0),
            jnp.stack(pool_s, axis=0),
            jnp.stack(v_s, axis=0))
```

```python
import functools

import jax
import jax.numpy as jnp
from jax import lax
from jax.experimental import pallas as pl
from jax.experimental.pallas import tpu as pltpu

D_MODEL = 1024
DEPTH = 4
PAST_LEN = 16384
D_POOL = 512
N_POOL_GROUPS = 4
POOL_GROUP_DIM = 128
POOL_WINDOWS = (2, 4, 8, 16)
POOL_BUF = 15
HIST = 16
D_SGU = 512
SGU_HEADS = 4
SGU_HEAD_DIM = 128
CHUNK = 128
N_GROUPS = 4
EXPERTS_PER_GROUP = 8
N_EXPERTS = 32
D_EXPERT = 512
DEEPNORM_ALPHA = (2.0 * DEPTH) ** 0.25
LN_EPS = 1e-5

LANES = 128
SUB = 8
assert D_MODEL == SUB * LANES
ROUTE_G_LANE = N_EXPERTS
SECOND_LANE = 64

TM_MIX = 512
TM_ROW = 256
TM_DISPATCH = 384
BM = 256
ROW_BITS = 16

F32 = jnp.float32
BF16 = jnp.bfloat16
VMEM_LIMIT = 48 * 1024 * 1024


def _layer_norm(r, g, b):
    mu = jnp.mean(r, axis=-1, keepdims=True)
    rc = r - mu
    var = jnp.mean(rc * rc, axis=-1, keepdims=True)
    return rc * lax.rsqrt(var + LN_EPS) * g + b


def _head_norm(vh, g):
    mu = jnp.mean(vh, axis=-1, keepdims=True)
    vc = vh - mu
    var = jnp.mean(vc * vc, axis=-1, keepdims=True)
    return vc * lax.rsqrt(var + LN_EPS) * g


def _roll_half(row):
    return pltpu.roll(jnp.broadcast_to(row, (8, LANES)), SECOND_LANE, 1)[0:1]


def _store_row_tiles(ref, val):
    rows = val.shape[0]
    for c in range(SUB):
        ref[pl.ds(c, rows, stride=SUB), :] = val[:, c * LANES:(c + 1) * LANES]


def _load_row_tiles(ref, rows):
    return jnp.concatenate(
        [ref[pl.ds(c, rows, stride=SUB), :] for c in range(SUB)], axis=1)


def _route(logits, ltri_ref, run_ref):
    tm = logits.shape[0]
    lane = lax.broadcasted_iota(jnp.int32, (tm, LANES), 1)
    lanef = lane.astype(F32)
    neg = -jnp.inf
    is_g = (lane >= ROUTE_G_LANE) & (lane < ROUTE_G_LANE + N_GROUPS)
    glm = jnp.where(is_g, logits, neg)
    gmax = jnp.max(glm, axis=1, keepdims=True)
    g_idx = jnp.min(jnp.where(glm == gmax, lanef - ROUTE_G_LANE, 1e4), axis=1, keepdims=True)
    p_g = 1.0 / jnp.sum(jnp.exp(glm - gmax), axis=1, keepdims=True)

    in_grp = (lane < N_EXPERTS) & ((lane >> 3).astype(F32) == g_idx)
    elm = jnp.where(in_grp, logits, neg)
    m1 = jnp.max(elm, axis=1, keepdims=True)
    i1 = jnp.min(jnp.where(elm == m1, lanef, 1e4), axis=1, keepdims=True)
    elm2 = jnp.where(lanef == i1, neg, elm)
    m2 = jnp.max(elm2, axis=1, keepdims=True)
    i2 = jnp.min(jnp.where(elm2 == m2, lanef, 1e4), axis=1, keepdims=True)
    e21 = jnp.exp(m2 - m1)
    den = 1.0 + e21
    w0 = (1.0 / den) * p_g
    w1 = (e21 / den) * p_g

    hit0 = lanef == i1
    hit1 = lanef == i2 + SECOND_LANE
    oh = jnp.where(hit0 | hit1, 1.0, 0.0)
    before = jnp.dot(ltri_ref[...], oh.astype(BF16), preferred_element_type=F32)
    tot = jnp.sum(oh, axis=0, keepdims=True)
    lane1 = lax.broadcasted_iota(jnp.int32, (1, LANES), 1)
    tot0 = jnp.where(lane1 < N_EXPERTS, tot, 0.0)
    tot1 = jnp.where(lane1 >= SECOND_LANE, tot, 0.0)
    run = run_ref[...]
    base = run + _roll_half(run + tot0)
    val = before + base
    rank0 = jnp.sum(jnp.where(hit0, val, 0.0), axis=1, keepdims=True)
    rank1 = jnp.sum(jnp.where(hit1, val, 0.0), axis=1, keepdims=True)
    run_ref[...] = run + tot0 + _roll_half(tot1)

    out = jnp.where(lane == 0, i1, 0.0)
    out = jnp.where(lane == 1, i2, out)
    out = jnp.where(lane == 2, w0, out)
    out = jnp.where(lane == 3, w1, out)
    out = jnp.where(lane == 4, rank0, out)
    out = jnp.where(lane == 5, rank1, out)
    return out


def _finish_mix(x, mix, g1, sh2, sc2, ln1g_ref, ln1b_ref, wr_ref, br_ref, ltri_ref, run_ref,
                x1_ref, h2_ref, route_ref):
    r = DEEPNORM_ALPHA * x + (1.0 + g1) * mix
    x1 = _layer_norm(r, ln1g_ref[...], ln1b_ref[...])
    x1_ref[...] = x1
    h2 = x1 * (1.0 + sc2) + sh2
    _store_row_tiles(h2_ref, h2)
    hi = h2.astype(BF16)
    lo = (h2 - hi.astype(F32)).astype(BF16)
    both = jnp.dot(hi, wr_ref[...], preferred_element_type=F32)
    logits = (both[:, :LANES] + both[:, LANES:]
              + jnp.dot(lo, wr_ref[:, :LANES], preferred_element_type=F32) + br_ref[...])
    route_ref[...] = _route(logits, ltri_ref, run_ref)


def _masked_sgu_w(sguw_ref, hd):
    t = lax.broadcasted_iota(jnp.int32, (CHUNK, CHUNK), 0)
    s = lax.broadcasted_iota(jnp.int32, (CHUNK, CHUNK), 1)
    return jnp.where(s <= t, sguw_ref[hd], 0.0).astype(BF16)


def _full(shape):
    nd = len(shape)
    return pl.BlockSpec(shape, lambda *_: (0,) * nd)


_ANY = pl.BlockSpec(memory_space=pl.ANY)


def _mod_kernel(c_ref, w_ref, b_ref, o_ref):
    s = jax.nn.silu(c_ref[...])
    o_ref[...] = jnp.dot(s.astype(BF16), w_ref[...].astype(BF16),
                         preferred_element_type=F32) + b_ref[...]


def _mod_call(c_all, w_ada, b_ada):
    n = c_all.shape[0]
    tn = 1024
    return pl.pallas_call(
        _mod_kernel,
        grid=(DEPTH, 6 * D_MODEL // tn),
        in_specs=[
            pl.BlockSpec((n, D_MODEL), lambda l, j: (0, 0)),
            pl.BlockSpec((None, D_MODEL, tn), lambda l, j: (l, 0, j)),
            pl.BlockSpec((None, 1, tn), lambda l, j: (l, 0, j)),
        ],
        out_specs=pl.BlockSpec((None, n, tn), lambda l, j: (l, 0, j)),
        out_shape=jax.ShapeDtypeStruct((DEPTH, n, 6 * D_MODEL), F32),
        name="adaln_mod",
        compiler_params=pltpu.CompilerParams(
            dimension_semantics=("arbitrary", "arbitrary")),
    )(c_all, w_ada, b_ada.reshape(DEPTH, 1, 6 * D_MODEL))


def _mix_prompt_kernel(x_ref, mod_ref, win_ref, wout_ref, poolw_ref, pscale_ref, sgug_ref,
                       sguw_ref, sgub_ref, ln1g_ref, ln1b_ref, wr_ref, br_ref, ltri_ref,
                       h2_in_ref, route_in_ref,
                       x1_ref, h2_ref, route_ref, nbuf_ref, cnt_ref,
                       pe_ref, mixin_ref, run_ref, *, tm, tiles_per_seq):
    del h2_in_ref, route_in_ref
    b = pl.program_id(0)
    j = pl.program_id(1)

    @pl.when((b == 0) & (j == 0))
    def _():
        run_ref[...] = jnp.zeros_like(run_ref)

    @pl.when(j == 0)
    def _():
        pe_ref[0:HIST, :] = jnp.zeros((HIST, D_POOL), F32)

    x = x_ref[...]
    mod = mod_ref[...]
    sh1, sc1, g1, sh2, sc2 = (mod[i:i + 1] for i in range(5))
    h = x * (1.0 + sc1) + sh1
    proj = jnp.dot(h.astype(BF16), win_ref[...], preferred_element_type=F32)
    p = proj[:, :D_POOL]
    u = proj[:, D_POOL:D_POOL + D_SGU]
    v = proj[:, D_POOL + D_SGU:]

    pe_ref[HIST:HIST + tm, :] = p
    pos = j * tm + lax.broadcasted_iota(jnp.int32, (tm, 1), 0)
    for g, w in enumerate(POOL_WINDOWS):
        lo, hi = g * POOL_GROUP_DIM, (g + 1) * POOL_GROUP_DIM
        pg = p[:, lo:hi]
        s = pg
        for k in range(1, w):
            s = s + pe_ref[pl.ds(HIST - k, tm), lo:hi]
        cnt = jnp.minimum(pos + 1, w).astype(F32)
        d = s / cnt - pg
        a = jnp.dot(d.astype(BF16), poolw_ref[g], preferred_element_type=F32)
        mixin_ref[:, lo:hi] = (a * pscale_ref[:, lo:hi]).astype(BF16)

    @pl.when(j == tiles_per_seq - 1)
    def _():
        nbuf_ref[...] = pe_ref[tm:tm + HIST, :]

    pe_ref[0:HIST, :] = pe_ref[tm:tm + HIST, :]

    for hd in range(SGU_HEADS):
        lo, hi = hd * SGU_HEAD_DIM, (hd + 1) * SGU_HEAD_DIM
        vn = _head_norm(v[:, lo:hi], sgug_ref[:, lo:hi]).astype(BF16)
        ws = _masked_sgu_w(sguw_ref, hd)
        bcol = sgub_ref[:, hd:hd + 1]
        for c in range(tm // CHUNK):
            r0, r1 = c * CHUNK, (c + 1) * CHUNK
            z = jnp.dot(ws, vn[r0:r1], preferred_element_type=F32) + bcol
            mixin_ref[r0:r1, D_POOL + lo:D_POOL + hi] = (u[r0:r1, lo:hi] * z).astype(BF16)

    mix = jnp.dot(mixin_ref[...], wout_ref[...], preferred_element_type=F32)
    _finish_mix(x, mix, g1, sh2, sc2, ln1g_ref, ln1b_ref, wr_ref, br_ref, ltri_ref, run_ref,
                x1_ref, h2_ref, route_ref)
    cnt_ref[...] = run_ref[...]


def _mix_prompt_call(x_all, h2_buf, route_buf, mod_p, lw, n_seq, seq_len):
    tm = TM_MIX
    tps = seq_len // tm
    t_all = x_all.shape[0]
    row = lambda b, j: (b * tps + j, 0)
    return pl.pallas_call(
        functools.partial(_mix_prompt_kernel, tm=tm, tiles_per_seq=tps),
        grid=(n_seq, tps),
        in_specs=[
            pl.BlockSpec((tm, D_MODEL), row),
            pl.BlockSpec((None, 6, D_MODEL), lambda b, j: (b, 0, 0)),
            _full((D_MODEL, D_POOL + 2 * D_SGU)),
            _full((D_POOL + D_SGU, D_MODEL)),
            _full((N_POOL_GROUPS, POOL_GROUP_DIM, POOL_GROUP_DIM)),
            _full((1, D_POOL)),
            _full((1, D_SGU)),
            _full((SGU_HEADS, CHUNK, CHUNK)),
            _full((CHUNK, SGU_HEADS)),
            _full((1, D_MODEL)),
            _full((1, D_MODEL)),
            _full((D_MODEL, 2 * LANES)),
            _full((1, LANES)),
            _full((tm, tm)),
            _ANY, _ANY,
        ],
        out_specs=[
            pl.BlockSpec((tm, D_MODEL), row),
            pl.BlockSpec((tm * SUB, LANES), row),
            pl.BlockSpec((tm, LANES), row),
            pl.BlockSpec((None, HIST, D_POOL), lambda b, j: (b, 0, 0)),
            pl.BlockSpec((1, LANES), lambda b, j: (0, 0)),
        ],
        out_shape=[
            jax.ShapeDtypeStruct((t_all, D_MODEL), F32),
            jax.ShapeDtypeStruct((t_all * SUB, LANES), F32),
            jax.ShapeDtypeStruct((t_all, LANES), F32),
            jax.ShapeDtypeStruct((n_seq, HIST, D_POOL), F32),
            jax.ShapeDtypeStruct((1, LANES), F32),
        ],
        scratch_shapes=[
            pltpu.VMEM((HIST + tm, D_POOL), F32),
            pltpu.VMEM((tm, D_POOL + D_SGU), BF16),
            pltpu.VMEM((1, LANES), F32),
        ],
        input_output_aliases={0: 0, 14: 1, 15: 2},
        name="mix_prompt",
        compiler_params=pltpu.CompilerParams(
            dimension_semantics=("arbitrary", "arbitrary"),
            vmem_limit_bytes=VMEM_LIMIT),
    )(x_all, mod_p, lw["w_in"], lw["w_out"], lw["pool_w"], lw["pool_scale"], lw["sgu_g"],
      lw["sgu_w"], lw["sgu_b_t"], lw["ln1_g"], lw["ln1_b"], lw["w_r"], lw["b_r"], lw["ltri_p"],
      h2_buf, route_buf)


def _mix_sample_kernel(x_ref, mod_ref, hist_ref, win_ref, wout_ref, poolw_ref, pscale_ref,
                       sgug_ref, sguw0_ref, sgub0_ref, ln1g_ref, ln1b_ref, wr_ref, br_ref,
                       ltri_ref, cnt_in_ref, h2_in_ref, route_in_ref,
                       x1_ref, h2_ref, route_ref, nbuf_ref, vn_ref, cnt_ref,
                       mixin_ref, run_ref):
    del h2_in_ref, route_in_ref
    run_ref[...] = cnt_in_ref[...]
    x = x_ref[...]
    sh1, sc1, g1, sh2, sc2 = (mod_ref[:, i * D_MODEL:(i + 1) * D_MODEL] for i in range(5))
    h = x * (1.0 + sc1) + sh1
    proj = jnp.dot(h.astype(BF16), win_ref[...], preferred_element_type=F32)
    p = proj[:, :D_POOL]
    u = proj[:, D_POOL:D_POOL + D_SGU]
    v = proj[:, D_POOL + D_SGU:]

    for g, w in enumerate(POOL_WINDOWS):
        lo, hi = g * POOL_GROUP_DIM, (g + 1) * POOL_GROUP_DIM
        pg = p[:, lo:hi]
        s = pg
        for k in range(1, w):
            s = s + hist_ref[POOL_BUF - k, :, lo:hi]
        cnt = float(min(PAST_LEN + 1, w))
        d = s / cnt - pg
        a = jnp.dot(d.astype(BF16), poolw_ref[g], preferred_element_type=F32)
        mixin_ref[:, lo:hi] = (a * pscale_ref[:, lo:hi]).astype(BF16)

    for k in range(POOL_BUF - 1):
        nbuf_ref[k] = hist_ref[k + 1]
    nbuf_ref[POOL_BUF - 1] = p

    for hd in range(SGU_HEADS):
        lo, hi = hd * SGU_HEAD_DIM, (hd + 1) * SGU_HEAD_DIM
        vn = _head_norm(v[:, lo:hi], sgug_ref[:, lo:hi])
        vn_ref[:, lo:hi] = vn
        z = sguw0_ref[:, lo:hi].astype(BF16).astype(F32) * vn.astype(BF16).astype(F32) \
            + sgub0_ref[:, lo:hi]
        mixin_ref[:, D_POOL + lo:D_POOL + hi] = (u[:, lo:hi] * z).astype(BF16)

    mix = jnp.dot(mixin_ref[...], wout_ref[...], preferred_element_type=F32)
    _finish_mix(x, mix, g1, sh2, sc2, ln1g_ref, ln1b_ref, wr_ref, br_ref, ltri_ref, run_ref,
                x1_ref, h2_ref, route_ref)
    cnt_ref[...] = run_ref[...]


def _mix_sample_call(x1_all, h2_all, route_all, mod_s, hist, lw, cnt_in, n_dec, row0):
    blk = row0 // n_dec
    rows = lambda r, w: pl.BlockSpec((r, w), lambda i: (blk, 0))
    return pl.pallas_call(
        _mix_sample_kernel,
        grid=(1,),
        in_specs=[
            rows(n_dec, D_MODEL),
            _full((n_dec, 6 * D_MODEL)),
            _full((POOL_BUF, n_dec, D_POOL)),
            _full((D_MODEL, D_POOL + 2 * D_SGU)),
            _full((D_POOL + D_SGU, D_MODEL)),
            _full((N_POOL_GROUPS, POOL_GROUP_DIM, POOL_GROUP_DIM)),
            _full((1, D_POOL)),
            _full((1, D_SGU)),
            _full((1, D_SGU)),
            _full((1, D_SGU)),
            _full((1, D_MODEL)),
            _full((1, D_MODEL)),
            _full((D_MODEL, 2 * LANES)),
            _full((1, LANES)),
            _full((n_dec, n_dec)),
            _full((1, LANES)),
            _ANY, _ANY,
        ],
        out_specs=[
            rows(n_dec, D_MODEL),
            rows(n_dec * SUB, LANES),
            rows(n_dec, LANES),
            _full((POOL_BUF, n_dec, D_POOL)),
            _full((n_dec, D_SGU)),
            _full((1, LANES)),
        ],
        out_shape=[
            jax.ShapeDtypeStruct(x1_all.shape, F32),
            jax.ShapeDtypeStruct(h2_all.shape, F32),
            jax.ShapeDtypeStruct(route_all.shape, F32),
            jax.ShapeDtypeStruct((POOL_BUF, n_dec, D_POOL), F32),
            jax.ShapeDtypeStruct((n_dec, D_SGU), F32),
            jax.ShapeDtypeStruct((1, LANES), F32),
        ],
        scratch_shapes=[
            pltpu.VMEM((n_dec, D_POOL + D_SGU), BF16),
            pltpu.VMEM((1, LANES), F32),
        ],
        input_output_aliases={0: 0, 16: 1, 17: 2},
        name="mix_sample",
        compiler_params=pltpu.CompilerParams(
            dimension_semantics=("arbitrary",),
            vmem_limit_bytes=VMEM_LIMIT),
    )(x1_all, mod_s, hist, lw["w_in"], lw["w_out"], lw["pool_w"], lw["pool_scale"], lw["sgu_g"],
      lw["sgu_w0"], lw["sgu_b0"], lw["ln1_g"], lw["ln1_b"], lw["w_r"], lw["b_r"], lw["ltri_s"],
      cnt_in, h2_all, route_all)


def _tile_copy(src_ref, src_row, dst_ref, dst_row, sem):
    def tile(ref, row):
        start = row * SUB if isinstance(row, int) else pl.multiple_of(row * SUB, SUB)
        return ref.at[pl.ds(start, SUB), :]

    return pltpu.make_async_copy(tile(src_ref, src_row), tile(dst_ref, dst_row), sem)


def _dispatch_kernel(pends_ref, dest_ref, h2_ref, fill_ref, xs_ref, slots_ref, zero_ref, sem,
                     *, tm, n_blocks, min_used, choice_stride):
    i = pl.program_id(0)

    @pl.when(i == 0)
    def _():
        fill = pltpu.make_async_copy(fill_ref, slots_ref, sem)
        fill.start()
        fill.wait()
        zero_ref[...] = jnp.zeros_like(zero_ref)
        n_used = lax.shift_right_logical(pends_ref[N_EXPERTS - 1], BM.bit_length() - 1)

        def block_copy(b):
            first = pl.multiple_of(b * (BM * SUB), BM * SUB)
            return pltpu.make_async_copy(zero_ref, xs_ref.at[pl.ds(first, BM * SUB), :], sem)

        def last_block(e):
            return lax.shift_right_logical(pends_ref[e], BM.bit_length() - 1) - 1

        for act in ("start", "wait"):
            for e in range(N_EXPERTS):
                prev_end = pends_ref[e - 1] if e else 0
                pl.when(pends_ref[e] > prev_end)(
                    lambda e=e, act=act: getattr(block_copy(last_block(e)), act)())
            for b in range(min_used, n_blocks):
                pl.when(b >= n_used)(lambda b=b, act=act: getattr(block_copy(b), act)())

    tile_word = (i * tm) << ROW_BITS
    for r in range(tm):
        for k in range(2):
            slot = dest_ref[0, 2 * r + k]
            _tile_copy(h2_ref, r, xs_ref, slot, sem).start(priority=k)
            slots_ref[slot] = tile_word + ((k * choice_stride + r) << ROW_BITS)
    for r in range(tm):
        for k in range(2):
            _tile_copy(h2_ref, r, xs_ref, 0, sem).wait()


def _dispatch_call(pends, dest, h2_all, fill_words, n_blocks, min_used, choice_stride):
    tm = TM_DISPATCH
    n_tiles = dest.shape[0] // tm
    grid_spec = pltpu.PrefetchScalarGridSpec(
        num_scalar_prefetch=1,
        grid=(n_tiles,),
        in_specs=[
            pl.BlockSpec((None, 1, 2 * tm), lambda i, pe: (i, 0, 0), memory_space=pltpu.SMEM),
            pl.BlockSpec((tm * SUB, LANES), lambda i, pe: (i, 0)),
            _ANY,
        ],
        out_specs=[_ANY, pl.BlockSpec(memory_space=pltpu.SMEM)],
        scratch_shapes=[pltpu.VMEM((BM * SUB, LANES), F32), pltpu.SemaphoreType.DMA],
    )
    return pl.pallas_call(
        functools.partial(_dispatch_kernel, tm=tm, n_blocks=n_blocks, min_used=min_used,
                          choice_stride=choice_stride),
        grid_spec=grid_spec,
        out_shape=[jax.ShapeDtypeStruct((n_blocks * BM * SUB, LANES), F32),
                   jax.ShapeDtypeStruct(fill_words.shape, jnp.int32)],
        name="dispatch_rows",
        compiler_params=pltpu.CompilerParams(dimension_semantics=("arbitrary",)),
    )(pends, dest.reshape(n_tiles, 1, 2 * tm), h2_all, fill_words)


SCATTER_GROUPS = 4


def _expert_kernel(be_ref, bfirst_ref, nused_ref, bnext_ref, bslot_ref, slot_prev_ref, slot_ref,
                   xs_ref, wg_ref, wu_ref, wd_ref,
                   y2_ref, ybuf_ref, wgb_ref, wub_ref, wdb_ref, wgf_ref, wuf_ref, wdf_ref,
                   ssem, wsem, *, layer, dump_base, unowned_rows):
    i = pl.program_id(0)
    n_used = nused_ref[0]

    def start_scatter(words_ref, buf, rows=range(BM)):
        for r in rows:
            row = lax.shift_right_logical(words_ref[0, r], ROW_BITS)
            _tile_copy(ybuf_ref.at[buf], r, y2_ref, row, ssem.at[buf]).start(priority=r % 2)

    def wait_scatter(buf):
        for r in range(BM):
            _tile_copy(ybuf_ref.at[buf], r, y2_ref, 0, ssem.at[buf]).wait()

    @pl.when(i == 0)
    def _():
        ybuf_ref[...] = jnp.zeros_like(ybuf_ref)
        unowned = [row for lo, hi in unowned_rows for row in range(lo, hi)]
        assert len(unowned) <= BM
        for r, row in enumerate(unowned):
            _tile_copy(ybuf_ref.at[1], r, y2_ref, row, ssem.at[1]).start(priority=r % 2)
        for r, row in enumerate(unowned):
            _tile_copy(ybuf_ref.at[1], r, y2_ref, row, ssem.at[1]).wait()
        for buf in range(2):
            for r in range(BM):
                _tile_copy(ybuf_ref.at[buf], r, y2_ref, dump_base + buf * BM + r,
                           ssem.at[buf]).start(priority=r % 2)

    def weight_copies(e, slot):
        return [pltpu.make_async_copy(src.at[layer, e], dst.at[slot], wsem.at[slot])
                for src, dst in ((wg_ref, wgf_ref), (wu_ref, wuf_ref), (wd_ref, wdf_ref))]

    @pl.when(i == 0)
    def _():
        for c in weight_copies(be_ref[0], 0):
            c.start()

    @pl.when(bfirst_ref[i] == 1)
    def _():
        e = be_ref[i]
        slot = bslot_ref[i]
        for c in weight_copies(e, slot):
            c.wait()
        wgb_ref[...] = wgf_ref[slot].astype(BF16)
        wub_ref[...] = wuf_ref[slot].astype(BF16)
        wdb_ref[...] = wdf_ref[slot].astype(BF16)

        @pl.when(bnext_ref[i] != e)
        def _():
            for c in weight_copies(bnext_ref[i], 1 - slot):
                c.start()

    def step(buf, scatter_prev):
        per_group = BM // SCATTER_GROUPS
        groups = iter(range(g * per_group, (g + 1) * per_group) for g in range(SCATTER_GROUPS))

        def scatter_piece():
            if scatter_prev:
                start_scatter(slot_prev_ref, 1 - buf, next(groups))

        x = _load_row_tiles(xs_ref, BM).astype(BF16)
        half = D_EXPERT // 2
        acts = []
        for lo in (0, half):
            g = jnp.dot(x, wgb_ref[:, lo:lo + half], preferred_element_type=F32)
            u = jnp.dot(x, wub_ref[:, lo:lo + half], preferred_element_type=F32)
            acts.append((jax.nn.silu(g) * u).astype(BF16))
            scatter_piece()
        a = jnp.concatenate(acts, axis=1)
        half = D_MODEL // 2
        ys = []
        for lo in (0, half):
            ys.append(jnp.dot(a, wdb_ref[:, lo:lo + half], preferred_element_type=F32))
            scatter_piece()
        y = jnp.concatenate(ys, axis=1)
        wait_scatter(buf)
        _store_row_tiles(ybuf_ref.at[buf], y)

        @pl.when(i == n_used - 1)
        def _():
            start_scatter(slot_ref, buf)
            wait_scatter(1 - buf)
            wait_scatter(buf)

    pl.when(i == 0)(functools.partial(step, 0, False))
    for buf in range(2):
        pl.when((i > 0) & (i < n_used) & (i % 2 == buf))(functools.partial(step, buf, True))


def _expert_call(layer, blk_e, blk_first, n_used, blk_next_e, blk_wslot, slots, xs,
                 w_gate, w_up, w_down, t_all, choice_stride):
    n_blocks = blk_e.shape[0]
    dump_base = 2 * choice_stride
    n_y_rows = dump_base + 2 * BM
    unowned_rows = ((t_all, choice_stride), (choice_stride + t_all, 2 * choice_stride))
    used = lambda i, nu: jnp.minimum(i, nu[0] - 1)
    grid_spec = pltpu.PrefetchScalarGridSpec(
        num_scalar_prefetch=5,
        grid=(n_blocks,),
        in_specs=[
            pl.BlockSpec((None, 1, BM),
                         lambda i, be, bf, nu, bn, bs: (jnp.clip(i - 1, 0, nu[0] - 1), 0, 0),
                         memory_space=pltpu.SMEM),
            pl.BlockSpec((None, 1, BM), lambda i, be, bf, nu, bn, bs: (used(i, nu), 0, 0),
                         memory_space=pltpu.SMEM),
            pl.BlockSpec((BM * SUB, LANES), lambda i, be, bf, nu, bn, bs: (used(i, nu), 0)),
            _ANY, _ANY, _ANY,
        ],
        out_specs=_ANY,
        scratch_shapes=[
            pltpu.VMEM((2, BM * SUB, LANES), F32),
            pltpu.VMEM((D_MODEL, D_EXPERT), BF16),
            pltpu.VMEM((D_MODEL, D_EXPERT), BF16),
            pltpu.VMEM((D_EXPERT, D_MODEL), BF16),
            pltpu.VMEM((2, D_MODEL, D_EXPERT), F32),
            pltpu.VMEM((2, D_MODEL, D_EXPERT), F32),
            pltpu.VMEM((2, D_EXPERT, D_MODEL), F32),
            pltpu.SemaphoreType.DMA((2,)),
            pltpu.SemaphoreType.DMA((2,)),
        ],
    )
    return pl.pallas_call(
        functools.partial(_expert_kernel, layer=layer, dump_base=dump_base,
                          unowned_rows=unowned_rows),
        grid_spec=grid_spec,
        out_shape=jax.ShapeDtypeStruct((n_y_rows * SUB, LANES), F32),
        name="expert_mlp",
        compiler_params=pltpu.CompilerParams(
            dimension_semantics=("arbitrary",),
            vmem_limit_bytes=VMEM_LIMIT),
    )(blk_e, blk_first, n_used, blk_next_e, blk_wslot, slots, slots, xs, w_gate, w_up, w_down)


def _combine_kernel(x1_ref, route_ref, y0_ref, y1_ref, g2_ref, ln2g_ref, ln2b_ref, out_ref, *, tm):
    route = route_ref[...]
    f = route[:, 2:3] * _load_row_tiles(y0_ref, tm) + route[:, 3:4] * _load_row_tiles(y1_ref, tm)
    r = DEEPNORM_ALPHA * x1_ref[...] + (1.0 + g2_ref[...]) * f
    out_ref[...] = _layer_norm(r, ln2g_ref[...], ln2b_ref[...])


def _combine_call(x1_all, route_all, y2, g2, g2_spec, ln2_g, ln2_b, tm, row0, n_rows,
                  choice_stride, in_place):
    blk0 = row0 // tm
    blk1 = (choice_stride + row0) // tm
    assert row0 % tm == 0 and n_rows % tm == 0 and choice_stride % tm == 0
    rows = lambda i: (blk0 + i, 0)
    if in_place:
        out_spec = pl.BlockSpec((tm, D_MODEL), rows)
        out_shape = jax.ShapeDtypeStruct(x1_all.shape, F32)
        aliases = {0: 0}
    else:
        out_spec = pl.BlockSpec((tm, D_MODEL), lambda i: (i, 0))
        out_shape = jax.ShapeDtypeStruct((n_rows, D_MODEL), F32)
        aliases = {}
    return pl.pallas_call(
        functools.partial(_combine_kernel, tm=tm),
        grid=(n_rows // tm,),
        in_specs=[
            pl.BlockSpec((tm, D_MODEL), rows),
            pl.BlockSpec((tm, LANES), rows),
            pl.BlockSpec((tm * SUB, LANES), rows),
            pl.BlockSpec((tm * SUB, LANES), lambda i: (blk1 + i, 0)),
            g2_spec,
            _full((1, D_MODEL)),
            _full((1, D_MODEL)),
        ],
        out_specs=out_spec,
        out_shape=out_shape,
        input_output_aliases=aliases,
        name="combine_ln2",
        compiler_params=pltpu.CompilerParams(
            dimension_semantics=("arbitrary",),
            vmem_limit_bytes=VMEM_LIMIT),
    )(x1_all, route_all, y2, y2, g2, ln2_g, ln2_b)


def _split_bf16(w):
    hi = w.astype(BF16)
    lo = (w - hi.astype(F32)).astype(BF16)
    return jnp.concatenate([hi, lo], axis=1)


def _strict_lower(n):
    r = lax.broadcasted_iota(jnp.int32, (n, n), 0)
    c = lax.broadcasted_iota(jnp.int32, (n, n), 1)
    return (c < r).astype(BF16)


def kernel(x_prompt, x_sample, state_pool, c_prompt, c_sample, w_ada, b_ada, w_in, pool_w, pool_scale, sgu_norm_g, sgu_w, sgu_b, w_out, ln1_g, ln1_b, router_g_w, router_g_b, router_e_w, router_e_b, exp_w_gate, exp_w_up, exp_w_down, ln2_g, ln2_b):
    n_seq, seq_len, _ = x_prompt.shape
    n_dec = x_sample.shape[0]
    t_p = n_seq * seq_len
    t_all = t_p + n_dec
    assert t_p % n_dec == 0 and seq_len % TM_MIX == 0 and seq_len % TM_ROW == 0
    assert t_all % TM_DISPATCH == 0
    n_assign = 2 * t_all
    n_blocks = -(-n_assign // BM) + N_EXPERTS
    n_slots = n_blocks * BM
    choice_stride = -(-t_all // TM_ROW) * TM_ROW
    dump_base = 2 * choice_stride
    n_y_rows = dump_base + 2 * BM
    assert n_y_rows <= (1 << (32 - ROW_BITS)) and t_all <= (1 << ROW_BITS)

    x_all = jnp.concatenate([x_prompt.reshape(t_p, D_MODEL), x_sample.reshape(n_dec, D_MODEL)])
    mod = _mod_call(jnp.concatenate([c_prompt, c_sample], axis=0), w_ada, b_ada)
    ltri_p = _strict_lower(TM_MIX)
    ltri_s = _strict_lower(n_dec)
    lane_pad = jnp.zeros((D_MODEL, LANES - N_EXPERTS - N_GROUPS), F32)
    expert_ids = jnp.arange(N_EXPERTS, dtype=jnp.int32)
    block_start = jnp.arange(n_blocks, dtype=jnp.int32) * BM
    fill_words = (dump_base + (jnp.arange(n_slots, dtype=jnp.int32) & (2 * BM - 1))) << ROW_BITS

    h2_all = jnp.zeros((t_all * SUB, LANES), F32)
    route_all = jnp.zeros((t_all, LANES), F32)

    pool_p, pool_s, v_s = [], [], []
    for l in range(DEPTH):
        lw = {
            "w_in": w_in[l].astype(BF16),
            "w_out": w_out[l].astype(BF16),
            "pool_w": pool_w[l].astype(BF16),
            "pool_scale": pool_scale[l].reshape(1, D_POOL),
            "sgu_g": sgu_norm_g[l].reshape(1, D_SGU),
            "sgu_w": sgu_w[l],
            "sgu_b_t": sgu_b[l].T,
            "sgu_w0": jnp.repeat(sgu_w[l, :, 0, 0], SGU_HEAD_DIM).reshape(1, D_SGU),
            "sgu_b0": jnp.repeat(sgu_b[l, :, 0], SGU_HEAD_DIM).reshape(1, D_SGU),
            "ln1_g": ln1_g[l].reshape(1, D_MODEL),
            "ln1_b": ln1_b[l].reshape(1, D_MODEL),
            "w_r": _split_bf16(jnp.concatenate([router_e_w[l], router_g_w[l], lane_pad], axis=1)),
            "b_r": jnp.concatenate([router_e_b[l], router_g_b[l],
                                    jnp.zeros((LANES - N_EXPERTS - N_GROUPS,), F32)]).reshape(1, LANES),
            "ltri_p": ltri_p,
            "ltri_s": ltri_s,
        }
        mod_p = mod[l, :n_seq].reshape(n_seq, 6, D_MODEL)
        mod_s = mod[l, n_seq:]

        x1_all, h2_all, route_all, nbuf_p, cnt_p = _mix_prompt_call(
            x_all, h2_all, route_all, mod_p, lw, n_seq, seq_len)
        hist = jnp.transpose(state_pool[l], (1, 0, 2))
        x1_all, h2_all, route_all, nbuf_s, vn_s, cnt = _mix_sample_call(
            x1_all, h2_all, route_all, mod_s, hist, lw, cnt_p, n_dec, t_p)
        pool_p.append(nbuf_p[:, HIST - POOL_BUF:])
        pool_s.append(jnp.transpose(nbuf_s, (1, 0, 2)))
        v_s.append(vn_s.reshape(n_dec, 1, D_SGU))

        counts = cnt[0, :N_EXPERTS].astype(jnp.int32)
        padded = ((counts + BM - 1) // BM) * BM
        pends = jnp.cumsum(padded)
        pstarts = pends - padded
        e_sel = route_all[:, 0:2].astype(jnp.int32)
        rank = route_all[:, 4:6].astype(jnp.int32)
        dest = rank + jnp.sum(jnp.where(e_sel[..., None] == expert_ids, pstarts, 0), axis=-1)
        blk_valid = block_start < pends[-1]
        n_used = (pends[-1:] // BM).astype(jnp.int32)
        blk_e = jnp.sum((pends[None, :] <= block_start[:, None]).astype(jnp.int32), axis=1)
        blk_e = jnp.minimum(blk_e, N_EXPERTS - 1)
        last_e = jnp.max(jnp.where(blk_valid, blk_e, 0))
        blk_e = jnp.where(blk_valid, blk_e, last_e).astype(jnp.int32)
        blk_first = (blk_valid & jnp.any(pstarts[None, :] == block_start[:, None], axis=1)
                     ).astype(jnp.int32)
        has_tokens = counts > 0
        later = (expert_ids[None, :] > expert_ids[:, None]) & has_tokens[None, :]
        next_e = jnp.min(jnp.where(later, expert_ids[None, :], N_EXPERTS), axis=1)
        next_e = jnp.where(next_e == N_EXPERTS, expert_ids, next_e)
        wslot = (jnp.cumsum(has_tokens.astype(jnp.int32)) - 1) & 1
        blk_next_e = next_e[blk_e].astype(jnp.int32)
        blk_wslot = wslot[blk_e].astype(jnp.int32)

        xs, slots = _dispatch_call(pends, dest, h2_all, fill_words, n_blocks, n_assign // BM,
                                   choice_stride)
        y2 = _expert_call(l, blk_e, blk_first, n_used, blk_next_e, blk_wslot,
                          slots.reshape(n_blocks, 1, BM), xs,
                          exp_w_gate, exp_w_up, exp_w_down, t_all, choice_stride)

        ln2g = ln2_g[l].reshape(1, D_MODEL)
        ln2b = ln2_b[l].reshape(1, D_MODEL)
        tiles_per_seq = seq_len // TM_ROW
        g2_p_spec = pl.BlockSpec((None, 1, D_MODEL), lambda i: (i // tiles_per_seq, 0, 0))
        last = l == DEPTH - 1
        out_p = _combine_call(x1_all, route_all, y2, mod_p[:, 5:6, :], g2_p_spec, ln2g, ln2b,
                              TM_ROW, 0, t_p, choice_stride, in_place=not last)
        out_s = _combine_call(x1_all if last else out_p, route_all, y2, mod_s[:, 5 * D_MODEL:],
                              _full((n_dec, D_MODEL)), ln2g, ln2b,
                              n_dec, t_p, n_dec, choice_stride, in_place=not last)
        x_all = out_s

    return (out_p.reshape(n_seq, seq_len, D_MODEL),
            out_s.reshape(n_dec, 1, D_MODEL),
            jnp.stack(pool_p, axis=0),
            jnp.stack(pool_s, axis=0),
            jnp.stack(v_s, axis=0))
```

```python
import functools

import jax
import jax.numpy as jnp
from jax import lax
from jax.experimental import pallas as pl
from jax.experimental.pallas import tpu as pltpu

D_MODEL = 1024
DEPTH = 4
PAST_LEN = 16384
D_POOL = 512
N_POOL_GROUPS = 4
POOL_GROUP_DIM = 128
POOL_WINDOWS = (2, 4, 8, 16)
POOL_BUF = 15
HIST = 16
D_SGU = 512
SGU_HEADS = 4
SGU_HEAD_DIM = 128
CHUNK = 128
N_GROUPS = 4
EXPERTS_PER_GROUP = 8
N_EXPERTS = 32
D_EXPERT = 512
DEEPNORM_ALPHA = (2.0 * DEPTH) ** 0.25
LN_EPS = 1e-5

LANES = 128
SUB = 8
assert D_MODEL == SUB * LANES
ROUTE_G_LANE = N_EXPERTS
SECOND_LANE = 64

TM_MIX = 512
TM_ROW = 256
TM_DISPATCH = 384
BM = 256
ROW_BITS = 16

F32 = jnp.float32
BF16 = jnp.bfloat16
VMEM_LIMIT = 48 * 1024 * 1024


def _layer_norm(r, g, b):
    mu = jnp.mean(r, axis=-1, keepdims=True)
    rc = r - mu
    var = jnp.mean(rc * rc, axis=-1, keepdims=True)
    return rc * lax.rsqrt(var + LN_EPS) * g + b


def _head_norm(vh, g):
    mu = jnp.mean(vh, axis=-1, keepdims=True)
    vc = vh - mu
    var = jnp.mean(vc * vc, axis=-1, keepdims=True)
    return vc * lax.rsqrt(var + LN_EPS) * g


def _roll_half(row):
    return pltpu.roll(jnp.broadcast_to(row, (8, LANES)), SECOND_LANE, 1)[0:1]


def _store_row_tiles(ref, val):
    rows = val.shape[0]
    for c in range(SUB):
        ref[pl.ds(c, rows, stride=SUB), :] = val[:, c * LANES:(c + 1) * LANES]


def _load_row_tiles(ref, rows):
    return jnp.concatenate(
        [ref[pl.ds(c, rows, stride=SUB), :] for c in range(SUB)], axis=1)


def _route(logits, ltri_ref, run_ref):
    tm = logits.shape[0]
    lane = lax.broadcasted_iota(jnp.int32, (tm, LANES), 1)
    lanef = lane.astype(F32)
    neg = -jnp.inf
    is_g = (lane >= ROUTE_G_LANE) & (lane < ROUTE_G_LANE + N_GROUPS)
    glm = jnp.where(is_g, logits, neg)
    gmax = jnp.max(glm, axis=1, keepdims=True)
    g_idx = jnp.min(jnp.where(glm == gmax, lanef - ROUTE_G_LANE, 1e4), axis=1, keepdims=True)
    p_g = 1.0 / jnp.sum(jnp.exp(glm - gmax), axis=1, keepdims=True)

    in_grp = (lane < N_EXPERTS) & ((lane >> 3).astype(F32) == g_idx)
    elm = jnp.where(in_grp, logits, neg)
    m1 = jnp.max(elm, axis=1, keepdims=True)
    i1 = jnp.min(jnp.where(elm == m1, lanef, 1e4), axis=1, keepdims=True)
    elm2 = jnp.where(lanef == i1, neg, elm)
    m2 = jnp.max(elm2, axis=1, keepdims=True)
    i2 = jnp.min(jnp.where(elm2 == m2, lanef, 1e4), axis=1, keepdims=True)
    e21 = jnp.exp(m2 - m1)
    den = 1.0 + e21
    w0 = (1.0 / den) * p_g
    w1 = (e21 / den) * p_g

    hit0 = lanef == i1
    hit1 = lanef == i2 + SECOND_LANE
    oh = jnp.where(hit0 | hit1, 1.0, 0.0)
    before = jnp.dot(ltri_ref[...], oh.astype(BF16), preferred_element_type=F32)
    tot = jnp.sum(oh, axis=0, keepdims=True)
    lane1 = lax.broadcasted_iota(jnp.int32, (1, LANES), 1)
    tot0 = jnp.where(lane1 < N_EXPERTS, tot, 0.0)
    tot1 = jnp.where(lane1 >= SECOND_LANE, tot, 0.0)
    run = run_ref[...]
    base = run + _roll_half(run + tot0)
    val = before + base
    rank0 = jnp.sum(jnp.where(hit0, val, 0.0), axis=1, keepdims=True)
    rank1 = jnp.sum(jnp.where(hit1, val, 0.0), axis=1, keepdims=True)
    run_ref[...] = run + tot0 + _roll_half(tot1)

    out = jnp.where(lane == 0, i1, 0.0)
    out = jnp.where(lane == 1, i2, out)
    out = jnp.where(lane == 2, w0, out)
    out = jnp.where(lane == 3, w1, out)
    out = jnp.where(lane == 4, rank0, out)
    out = jnp.where(lane == 5, rank1, out)
    return out


def _finish_mix(x, mix, g1, sh2, sc2, ln1g_ref, ln1b_ref, wr_ref, br_ref, ltri_ref, run_ref,
                x1_ref, h2_ref, route_ref):
    r = DEEPNORM_ALPHA * x + (1.0 + g1) * mix
    x1 = _layer_norm(r, ln1g_ref[...], ln1b_ref[...])
    x1_ref[...] = x1
    h2 = x1 * (1.0 + sc2) + sh2
    _store_row_tiles(h2_ref, h2)
    hi = h2.astype(BF16)
    lo = (h2 - hi.astype(F32)).astype(BF16)
    both = jnp.dot(hi, wr_ref[...], preferred_element_type=F32)
    logits = (both[:, :LANES] + both[:, LANES:]
              + jnp.dot(lo, wr_ref[:, :LANES], preferred_element_type=F32) + br_ref[...])
    route_ref[...] = _route(logits, ltri_ref, run_ref)


def _masked_sgu_w(sguw_ref, hd):
    t = lax.broadcasted_iota(jnp.int32, (CHUNK, CHUNK), 0)
    s = lax.broadcasted_iota(jnp.int32, (CHUNK, CHUNK), 1)
    return jnp.where(s <= t, sguw_ref[hd], 0.0).astype(BF16)


def _full(shape):
    nd = len(shape)
    return pl.BlockSpec(shape, lambda *_: (0,) * nd)


_ANY = pl.BlockSpec(memory_space=pl.ANY)


def _mod_kernel(c_ref, w_ref, b_ref, o_ref):
    s = jax.nn.silu(c_ref[...])
    o_ref[...] = jnp.dot(s.astype(BF16), w_ref[...].astype(BF16),
                         preferred_element_type=F32) + b_ref[...]


def _mod_call(c_all, w_ada, b_ada):
    n = c_all.shape[0]
    tn = 1024
    return pl.pallas_call(
        _mod_kernel,
        grid=(DEPTH, 6 * D_MODEL // tn),
        in_specs=[
            pl.BlockSpec((n, D_MODEL), lambda l, j: (0, 0)),
            pl.BlockSpec((None, D_MODEL, tn), lambda l, j: (l, 0, j)),
            pl.BlockSpec((None, 1, tn), lambda l, j: (l, 0, j)),
        ],
        out_specs=pl.BlockSpec((None, n, tn), lambda l, j: (l, 0, j)),
        out_shape=jax.ShapeDtypeStruct((DEPTH, n, 6 * D_MODEL), F32),
        name="adaln_mod",
        compiler_params=pltpu.CompilerParams(
            dimension_semantics=("arbitrary", "arbitrary")),
    )(c_all, w_ada, b_ada.reshape(DEPTH, 1, 6 * D_MODEL))


def _mix_prompt_kernel(x_ref, mod_ref, win_ref, wout_ref, poolw_ref, pscale_ref, sgug_ref,
                       sguw_ref, sgub_ref, ln1g_ref, ln1b_ref, wr_ref, br_ref, ltri_ref,
                       h2_in_ref, route_in_ref,
                       x1_ref, h2_ref, route_ref, nbuf_ref, cnt_ref,
                       pe_ref, mixin_ref, run_ref, *, tm, tiles_per_seq):
    del h2_in_ref, route_in_ref
    b = pl.program_id(0)
    j = pl.program_id(1)

    @pl.when((b == 0) & (j == 0))
    def _():
        run_ref[...] = jnp.zeros_like(run_ref)

    @pl.when(j == 0)
    def _():
        pe_ref[0:HIST, :] = jnp.zeros((HIST, D_POOL), F32)

    x = x_ref[...]
    mod = mod_ref[...]
    sh1, sc1, g1, sh2, sc2 = (mod[i:i + 1] for i in range(5))
    h = x * (1.0 + sc1) + sh1
    proj = jnp.dot(h.astype(BF16), win_ref[...], preferred_element_type=F32)
    p = proj[:, :D_POOL]
    u = proj[:, D_POOL:D_POOL + D_SGU]
    v = proj[:, D_POOL + D_SGU:]

    pe_ref[HIST:HIST + tm, :] = p
    pos = j * tm + lax.broadcasted_iota(jnp.int32, (tm, 1), 0)
    for g, w in enumerate(POOL_WINDOWS):
        lo, hi = g * POOL_GROUP_DIM, (g + 1) * POOL_GROUP_DIM
        pg = p[:, lo:hi]
        s = pg
        for k in range(1, w):
            s = s + pe_ref[pl.ds(HIST - k, tm), lo:hi]
        cnt = jnp.minimum(pos + 1, w).astype(F32)
        d = s / cnt - pg
        a = jnp.dot(d.astype(BF16), poolw_ref[g], preferred_element_type=F32)
        mixin_ref[:, lo:hi] = (a * pscale_ref[:, lo:hi]).astype(BF16)

    @pl.when(j == tiles_per_seq - 1)
    def _():
        nbuf_ref[...] = pe_ref[tm:tm + HIST, :]

    pe_ref[0:HIST, :] = pe_ref[tm:tm + HIST, :]

    for hd in range(SGU_HEADS):
        lo, hi = hd * SGU_HEAD_DIM, (hd + 1) * SGU_HEAD_DIM
        vn = _head_norm(v[:, lo:hi], sgug_ref[:, lo:hi]).astype(BF16)
        ws = _masked_sgu_w(sguw_ref, hd)
        bcol = sgub_ref[:, hd:hd + 1]
        for c in range(tm // CHUNK):
            r0, r1 = c * CHUNK, (c + 1) * CHUNK
            z = jnp.dot(ws, vn[r0:r1], preferred_element_type=F32) + bcol
            mixin_ref[r0:r1, D_POOL + lo:D_POOL + hi] = (u[r0:r1, lo:hi] * z).astype(BF16)

    mix = jnp.dot(mixin_ref[...], wout_ref[...], preferred_element_type=F32)
    _finish_mix(x, mix, g1, sh2, sc2, ln1g_ref, ln1b_ref, wr_ref, br_ref, ltri_ref, run_ref,
                x1_ref, h2_ref, route_ref)
    cnt_ref[...] = run_ref[...]


def _mix_prompt_call(x_all, h2_buf, route_buf, mod_p, lw, n_seq, seq_len):
    tm = TM_MIX
    tps = seq_len // tm
    t_all = x_all.shape[0]
    row = lambda b, j: (b * tps + j, 0)
    return pl.pallas_call(
        functools.partial(_mix_prompt_kernel, tm=tm, tiles_per_seq=tps),
        grid=(n_seq, tps),
        in_specs=[
            pl.BlockSpec((tm, D_MODEL), row),
            pl.BlockSpec((None, 6, D_MODEL), lambda b, j: (b, 0, 0)),
            _full((D_MODEL, D_POOL + 2 * D_SGU)),
            _full((D_POOL + D_SGU, D_MODEL)),
            _full((N_POOL_GROUPS, POOL_GROUP_DIM, POOL_GROUP_DIM)),
            _full((1, D_POOL)),
            _full((1, D_SGU)),
            _full((SGU_HEADS, CHUNK, CHUNK)),
            _full((CHUNK, SGU_HEADS)),
            _full((1, D_MODEL)),
            _full((1, D_MODEL)),
            _full((D_MODEL, 2 * LANES)),
            _full((1, LANES)),
            _full((tm, tm)),
            _ANY, _ANY,
        ],
        out_specs=[
            pl.BlockSpec((tm, D_MODEL), row),
            pl.BlockSpec((tm * SUB, LANES), row),
            pl.BlockSpec((tm, LANES), row),
            pl.BlockSpec((None, HIST, D_POOL), lambda b, j: (b, 0, 0)),
            pl.BlockSpec((1, LANES), lambda b, j: (0, 0)),
        ],
        out_shape=[
            jax.ShapeDtypeStruct((t_all, D_MODEL), F32),
            jax.ShapeDtypeStruct((t_all * SUB, LANES), F32),
            jax.ShapeDtypeStruct((t_all, LANES), F32),
            jax.ShapeDtypeStruct((n_seq, HIST, D_POOL), F32),
            jax.ShapeDtypeStruct((1, LANES), F32),
        ],
        scratch_shapes=[
            pltpu.VMEM((HIST + tm, D_POOL), F32),
            pltpu.VMEM((tm, D_POOL + D_SGU), BF16),
            pltpu.VMEM((1, LANES), F32),
        ],
        input_output_aliases={0: 0, 14: 1, 15: 2},
        name="mix_prompt",
        compiler_params=pltpu.CompilerParams(
            dimension_semantics=("arbitrary", "arbitrary"),
            vmem_limit_bytes=VMEM_LIMIT),
    )(x_all, mod_p, lw["w_in"], lw["w_out"], lw["pool_w"], lw["pool_scale"], lw["sgu_g"],
      lw["sgu_w"], lw["sgu_b_t"], lw["ln1_g"], lw["ln1_b"], lw["w_r"], lw["b_r"], lw["ltri_p"],
      h2_buf, route_buf)


def _mix_sample_kernel(x_ref, mod_ref, hist_ref, win_ref, wout_ref, poolw_ref, pscale_ref,
                       sgug_ref, sguw0_ref, sgub0_ref, ln1g_ref, ln1b_ref, wr_ref, br_ref,
                       ltri_ref, cnt_in_ref, h2_in_ref, route_in_ref,
                       x1_ref, h2_ref, route_ref, nbuf_ref, vn_ref, cnt_ref,
                       mixin_ref, run_ref):
    del h2_in_ref, route_in_ref
    run_ref[...] = cnt_in_ref[...]
    x = x_ref[...]
    sh1, sc1, g1, sh2, sc2 = (mod_ref[:, i * D_MODEL:(i + 1) * D_MODEL] for i in range(5))
    h = x * (1.0 + sc1) + sh1
    proj = jnp.dot(h.astype(BF16), win_ref[...], preferred_element_type=F32)
    p = proj[:, :D_POOL]
    u = proj[:, D_POOL:D_POOL + D_SGU]
    v = proj[:, D_POOL + D_SGU:]

    for g, w in enumerate(POOL_WINDOWS):
        lo, hi = g * POOL_GROUP_DIM, (g + 1) * POOL_GROUP_DIM
        pg = p[:, lo:hi]
        s = pg
        for k in range(1, w):
            s = s + hist_ref[POOL_BUF - k, :, lo:hi]
        cnt = float(min(PAST_LEN + 1, w))
        d = s / cnt - pg
        a = jnp.dot(d.astype(BF16), poolw_ref[g], preferred_element_type=F32)
        mixin_ref[:, lo:hi] = (a * pscale_ref[:, lo:hi]).astype(BF16)

    for k in range(POOL_BUF - 1):
        nbuf_ref[k] = hist_ref[k + 1]
    nbuf_ref[POOL_BUF - 1] = p

    for hd in range(SGU_HEADS):
        lo, hi = hd * SGU_HEAD_DIM, (hd + 1) * SGU_HEAD_DIM
        vn = _head_norm(v[:, lo:hi], sgug_ref[:, lo:hi])
        vn_ref[:, lo:hi] = vn
        z = sguw0_ref[:, lo:hi].astype(BF16).astype(F32) * vn.astype(BF16).astype(F32) \
            + sgub0_ref[:, lo:hi]
        mixin_ref[:, D_POOL + lo:D_POOL + hi] = (u[:, lo:hi] * z).astype(BF16)

    mix = jnp.dot(mixin_ref[...], wout_ref[...], preferred_element_type=F32)
    _finish_mix(x, mix, g1, sh2, sc2, ln1g_ref, ln1b_ref, wr_ref, br_ref, ltri_ref, run_ref,
                x1_ref, h2_ref, route_ref)
    cnt_ref[...] = run_ref[...]


def _mix_sample_call(x1_all, h2_all, route_all, mod_s, hist, lw, cnt_in, n_dec, row0):
    blk = row0 // n_dec
    rows = lambda r, w: pl.BlockSpec((r, w), lambda i: (blk, 0))
    return pl.pallas_call(
        _mix_sample_kernel,
        grid=(1,),
        in_specs=[
            rows(n_dec, D_MODEL),
            _full((n_dec, 6 * D_MODEL)),
            _full((POOL_BUF, n_dec, D_POOL)),
            _full((D_MODEL, D_POOL + 2 * D_SGU)),
            _full((D_POOL + D_SGU, D_MODEL)),
            _full((N_POOL_GROUPS, POOL_GROUP_DIM, POOL_GROUP_DIM)),
            _full((1, D_POOL)),
            _full((1, D_SGU)),
            _full((1, D_SGU)),
            _full((1, D_SGU)),
            _full((1, D_MODEL)),
            _full((1, D_MODEL)),
            _full((D_MODEL, 2 * LANES)),
            _full((1, LANES)),
            _full((n_dec, n_dec)),
            _full((1, LANES)),
            _ANY, _ANY,
        ],
        out_specs=[
            rows(n_dec, D_MODEL),
            rows(n_dec * SUB, LANES),
            rows(n_dec, LANES),
            _full((POOL_BUF, n_dec, D_POOL)),
            _full((n_dec, D_SGU)),
            _full((1, LANES)),
        ],
        out_shape=[
            jax.ShapeDtypeStruct(x1_all.shape, F32),
            jax.ShapeDtypeStruct(h2_all.shape, F32),
            jax.ShapeDtypeStruct(route_all.shape, F32),
            jax.ShapeDtypeStruct((POOL_BUF, n_dec, D_POOL), F32),
            jax.ShapeDtypeStruct((n_dec, D_SGU), F32),
            jax.ShapeDtypeStruct((1, LANES), F32),
        ],
        scratch_shapes=[
            pltpu.VMEM((n_dec, D_POOL + D_SGU), BF16),
            pltpu.VMEM((1, LANES), F32),
        ],
        input_output_aliases={0: 0, 16: 1, 17: 2},
        name="mix_sample",
        compiler_params=pltpu.CompilerParams(
            dimension_semantics=("arbitrary",),
            vmem_limit_bytes=VMEM_LIMIT),
    )(x1_all, mod_s, hist, lw["w_in"], lw["w_out"], lw["pool_w"], lw["pool_scale"], lw["sgu_g"],
      lw["sgu_w0"], lw["sgu_b0"], lw["ln1_g"], lw["ln1_b"], lw["w_r"], lw["b_r"], lw["ltri_s"],
      cnt_in, h2_all, route_all)


def _tile_copy(src_ref, src_row, dst_ref, dst_row, sem):
    def tile(ref, row):
        start = row * SUB if isinstance(row, int) else pl.multiple_of(row * SUB, SUB)
        return ref.at[pl.ds(start, SUB), :]

    return pltpu.make_async_copy(tile(src_ref, src_row), tile(dst_ref, dst_row), sem)


def _dispatch_kernel(pends_ref, dest_ref, h2_ref, fill_ref, xs_ref, slots_ref, zero_ref, sem,
                     *, tm, n_blocks, min_used, choice_stride):
    i = pl.program_id(0)

    @pl.when(i == 0)
    def _():
        fill = pltpu.make_async_copy(fill_ref, slots_ref, sem)
        fill.start()
        fill.wait()
        zero_ref[...] = jnp.zeros_like(zero_ref)
        n_used = lax.shift_right_logical(pends_ref[N_EXPERTS - 1], BM.bit_length() - 1)

        def block_copy(b):
            first = pl.multiple_of(b * (BM * SUB), BM * SUB)
            return pltpu.make_async_copy(zero_ref, xs_ref.at[pl.ds(first, BM * SUB), :], sem)

        def last_block(e):
            return lax.shift_right_logical(pends_ref[e], BM.bit_length() - 1) - 1

        for act in ("start", "wait"):
            for e in range(N_EXPERTS):
                prev_end = pends_ref[e - 1] if e else 0
                pl.when(pends_ref[e] > prev_end)(
                    lambda e=e, act=act: getattr(block_copy(last_block(e)), act)())
            for b in range(min_used, n_blocks):
                pl.when(b >= n_used)(lambda b=b, act=act: getattr(block_copy(b), act)())

    tile_word = (i * tm) << ROW_BITS
    for r in range(tm):
        for k in range(2):
            slot = dest_ref[0, 2 * r + k]
            _tile_copy(h2_ref, r, xs_ref, slot, sem).start(priority=k)
            slots_ref[slot] = tile_word + ((k * choice_stride + r) << ROW_BITS)
    for r in range(tm):
        for k in range(2):
            _tile_copy(h2_ref, r, xs_ref, 0, sem).wait()


def _dispatch_call(pends, dest, h2_all, fill_words, n_blocks, min_used, choice_stride):
    tm = TM_DISPATCH
    n_tiles = dest.shape[0] // tm
    grid_spec = pltpu.PrefetchScalarGridSpec(
        num_scalar_prefetch=1,
        grid=(n_tiles,),
        in_specs=[
            pl.BlockSpec((None, 1, 2 * tm), lambda i, pe: (i, 0, 0), memory_space=pltpu.SMEM),
            pl.BlockSpec((tm * SUB, LANES), lambda i, pe: (i, 0)),
            _ANY,
        ],
        out_specs=[_ANY, pl.BlockSpec(memory_space=pltpu.SMEM)],
        scratch_shapes=[pltpu.VMEM((BM * SUB, LANES), F32), pltpu.SemaphoreType.DMA],
    )
    return pl.pallas_call(
        functools.partial(_dispatch_kernel, tm=tm, n_blocks=n_blocks, min_used=min_used,
                          choice_stride=choice_stride),
        grid_spec=grid_spec,
        out_shape=[jax.ShapeDtypeStruct((n_blocks * BM * SUB, LANES), F32),
                   jax.ShapeDtypeStruct(fill_words.shape, jnp.int32)],
        name="dispatch_rows",
        compiler_params=pltpu.CompilerParams(dimension_semantics=("arbitrary",)),
    )(pends, dest.reshape(n_tiles, 1, 2 * tm), h2_all, fill_words)


SCATTER_GROUPS = 4


def _expert_kernel(be_ref, bfirst_ref, nused_ref, bnext_ref, bslot_ref, slot_prev_ref, slot_ref,
                   xs_ref, wg_ref, wu_ref, wd_ref,
                   y2_ref, ybuf_ref, wgb_ref, wub_ref, wdb_ref, wgf_ref, wuf_ref, wdf_ref,
                   ssem, wsem, *, layer, dump_base, unowned_rows):
    i = pl.program_id(0)
    n_used = nused_ref[0]

    def start_scatter(words_ref, buf, rows=range(BM)):
        for r in rows:
            row = lax.shift_right_logical(words_ref[0, r], ROW_BITS)
            _tile_copy(ybuf_ref.at[buf], r, y2_ref, row, ssem.at[buf]).start(priority=r % 2)

    def wait_scatter(buf):
        for r in range(BM):
            _tile_copy(ybuf_ref.at[buf], r, y2_ref, 0, ssem.at[buf]).wait()

    @pl.when(i == 0)
    def _():
        ybuf_ref[...] = jnp.zeros_like(ybuf_ref)
        unowned = [row for lo, hi in unowned_rows for row in range(lo, hi)]
        assert len(unowned) <= BM
        for r, row in enumerate(unowned):
            _tile_copy(ybuf_ref.at[1], r, y2_ref, row, ssem.at[1]).start(priority=r % 2)
        for r, row in enumerate(unowned):
            _tile_copy(ybuf_ref.at[1], r, y2_ref, row, ssem.at[1]).wait()
        for buf in range(2):
            for r in range(BM):
                _tile_copy(ybuf_ref.at[buf], r, y2_ref, dump_base + buf * BM + r,
                           ssem.at[buf]).start(priority=r % 2)

    def weight_copies(e, slot):
        return [pltpu.make_async_copy(src.at[layer, e], dst.at[slot], wsem.at[slot])
                for src, dst in ((wg_ref, wgf_ref), (wu_ref, wuf_ref), (wd_ref, wdf_ref))]

    @pl.when(i == 0)
    def _():
        for c in weight_copies(be_ref[0], 0):
            c.start()

    @pl.when(bfirst_ref[i] == 1)
    def _():
        e = be_ref[i]
        slot = bslot_ref[i]
        for c in weight_copies(e, slot):
            c.wait()
        wgb_ref[...] = wgf_ref[slot].astype(BF16)
        wub_ref[...] = wuf_ref[slot].astype(BF16)
        wdb_ref[...] = wdf_ref[slot].astype(BF16)

        @pl.when(bnext_ref[i] != e)
        def _():
            for c in weight_copies(bnext_ref[i], 1 - slot):
                c.start()

    def step(buf, scatter_prev):
        per_group = BM // SCATTER_GROUPS
        groups = iter(range(g * per_group, (g + 1) * per_group) for g in range(SCATTER_GROUPS))

        def scatter_piece():
            if scatter_prev:
                start_scatter(slot_prev_ref, 1 - buf, next(groups))

        x = _load_row_tiles(xs_ref, BM).astype(BF16)
        half = D_EXPERT // 2
        acts = []
        for lo in (0, half):
            g = jnp.dot(x, wgb_ref[:, lo:lo + half], preferred_element_type=F32)
            u = jnp.dot(x, wub_ref[:, lo:lo + half], preferred_element_type=F32)
            acts.append((jax.nn.silu(g) * u).astype(BF16))
            scatter_piece()
        a = jnp.concatenate(acts, axis=1)
        half = D_MODEL // 2
        ys = []
        for lo in (0, half):
            ys.append(jnp.dot(a, wdb_ref[:, lo:lo + half], preferred_element_type=F32))
            scatter_piece()
        y = jnp.concatenate(ys, axis=1)
        wait_scatter(buf)
        _store_row_tiles(ybuf_ref.at[buf], y)

        @pl.when(i == n_used - 1)
        def _():
            start_scatter(slot_ref, buf)
            wait_scatter(1 - buf)
            wait_scatter(buf)

    pl.when(i == 0)(functools.partial(step, 0, False))
    for buf in range(2):
        pl.when((i > 0) & (i < n_used) & (i % 2 == buf))(functools.partial(step, buf, True))


def _expert_call(layer, blk_e, blk_first, n_used, blk_next_e, blk_wslot, slots, xs,
                 w_gate, w_up, w_down, t_all, choice_stride):
    n_blocks = blk_e.shape[0]
    dump_base = 2 * choice_stride
    n_y_rows = dump_base + 2 * BM
    unowned_rows = ((t_all, choice_stride), (choice_stride + t_all, 2 * choice_stride))
    used = lambda i, nu: jnp.minimum(i, nu[0] - 1)
    grid_spec = pltpu.PrefetchScalarGridSpec(
        num_scalar_prefetch=5,
        grid=(n_blocks,),
        in_specs=[
            pl.BlockSpec((None, 1, BM),
                         lambda i, be, bf, nu, bn, bs: (jnp.clip(i - 1, 0, nu[0] - 1), 0, 0),
                         memory_space=pltpu.SMEM),
            pl.BlockSpec((None, 1, BM), lambda i, be, bf, nu, bn, bs: (used(i, nu), 0, 0),
                         memory_space=pltpu.SMEM),
            pl.BlockSpec((BM * SUB, LANES), lambda i, be, bf, nu, bn, bs: (used(i, nu), 0)),
            _ANY, _ANY, _ANY,
        ],
        out_specs=_ANY,
        scratch_shapes=[
            pltpu.VMEM((2, BM * SUB, LANES), F32),
            pltpu.VMEM((D_MODEL, D_EXPERT), BF16),
            pltpu.VMEM((D_MODEL, D_EXPERT), BF16),
            pltpu.VMEM((D_EXPERT, D_MODEL), BF16),
            pltpu.VMEM((2, D_MODEL, D_EXPERT), F32),
            pltpu.VMEM((2, D_MODEL, D_EXPERT), F32),
            pltpu.VMEM((2, D_EXPERT, D_MODEL), F32),
            pltpu.SemaphoreType.DMA((2,)),
            pltpu.SemaphoreType.DMA((2,)),
        ],
    )
    return pl.pallas_call(
        functools.partial(_expert_kernel, layer=layer, dump_base=dump_base,
                          unowned_rows=unowned_rows),
        grid_spec=grid_spec,
        out_shape=jax.ShapeDtypeStruct((n_y_rows * SUB, LANES), F32),
        name="expert_mlp",
        compiler_params=pltpu.CompilerParams(
            dimension_semantics=("arbitrary",),
            vmem_limit_bytes=VMEM_LIMIT),
    )(blk_e, blk_first, n_used, blk_next_e, blk_wslot, slots, slots, xs, w_gate, w_up, w_down)


def _combine_kernel(x1_ref, route_ref, y0_ref, y1_ref, g2_ref, ln2g_ref, ln2b_ref, out_ref, *, tm):
    route = route_ref[...]
    f = route[:, 2:3] * _load_row_tiles(y0_ref, tm) + route[:, 3:4] * _load_row_tiles(y1_ref, tm)
    r = DEEPNORM_ALPHA * x1_ref[...] + (1.0 + g2_ref[...]) * f
    out_ref[...] = _layer_norm(r, ln2g_ref[...], ln2b_ref[...])


def _combine_call(x1_all, route_all, y2, g2, g2_spec, ln2_g, ln2_b, tm, row0, n_rows,
                  choice_stride, in_place):
    blk0 = row0 // tm
    blk1 = (choice_stride + row0) // tm
    assert row0 % tm == 0 and n_rows % tm == 0 and choice_stride % tm == 0
    rows = lambda i: (blk0 + i, 0)
    if in_place:
        out_spec = pl.BlockSpec((tm, D_MODEL), rows)
        out_shape = jax.ShapeDtypeStruct(x1_all.shape, F32)
        aliases = {0: 0}
    else:
        out_spec = pl.BlockSpec((tm, D_MODEL), lambda i: (i, 0))
        out_shape = jax.ShapeDtypeStruct((n_rows, D_MODEL), F32)
        aliases = {}
    return pl.pallas_call(
        functools.partial(_combine_kernel, tm=tm),
        grid=(n_rows // tm,),
        in_specs=[
            pl.BlockSpec((tm, D_MODEL), rows),
            pl.BlockSpec((tm, LANES), rows),
            pl.BlockSpec((tm * SUB, LANES), rows),
            pl.BlockSpec((tm * SUB, LANES), lambda i: (blk1 + i, 0)),
            g2_spec,
            _full((1, D_MODEL)),
            _full((1, D_MODEL)),
        ],
        out_specs=out_spec,
        out_shape=out_shape,
        input_output_aliases=aliases,
        name="combine_ln2",
        compiler_params=pltpu.CompilerParams(
            dimension_semantics=("arbitrary",),
            vmem_limit_bytes=VMEM_LIMIT),
    )(x1_all, route_all, y2, y2, g2, ln2_g, ln2_b)


def _split_bf16(w):
    hi = w.astype(BF16)
    lo = (w - hi.astype(F32)).astype(BF16)
    return jnp.concatenate([hi, lo], axis=1)


def _strict_lower(n):
    r = lax.broadcasted_iota(jnp.int32, (n, n), 0)
    c = lax.broadcasted_iota(jnp.int32, (n, n), 1)
    return (c < r).astype(BF16)


def kernel(x_prompt, x_sample, state_pool, c_prompt, c_sample, w_ada, b_ada, w_in, pool_w, pool_scale, sgu_norm_g, sgu_w, sgu_b, w_out, ln1_g, ln1_b, router_g_w, router_g_b, router_e_w, router_e_b, exp_w_gate, exp_w_up, exp_w_down, ln2_g, ln2_b):
    n_seq, seq_len, _ = x_prompt.shape
    n_dec = x_sample.shape[0]
    t_p = n_seq * seq_len
    t_all = t_p + n_dec
    assert t_p % n_dec == 0 and seq_len % TM_MIX == 0 and seq_len % TM_ROW == 0
    assert t_all % TM_DISPATCH == 0
    n_assign = 2 * t_all
    n_blocks = -(-n_assign // BM) + N_EXPERTS
    n_slots = n_blocks * BM
    choice_stride = -(-t_all // TM_ROW) * TM_ROW
    dump_base = 2 * choice_stride
    n_y_rows = dump_base + 2 * BM
    assert n_y_rows <= (1 << (32 - ROW_BITS)) and t_all <= (1 << ROW_BITS)

    x_all = jnp.concatenate([x_prompt.reshape(t_p, D_MODEL), x_sample.reshape(n_dec, D_MODEL)])
    mod = _mod_call(jnp.concatenate([c_prompt, c_sample], axis=0), w_ada, b_ada)
    ltri_p = _strict_lower(TM_MIX)
    ltri_s = _strict_lower(n_dec)
    lane_pad = jnp.zeros((D_MODEL, LANES - N_EXPERTS - N_GROUPS), F32)
    expert_ids = jnp.arange(N_EXPERTS, dtype=jnp.int32)
    block_start = jnp.arange(n_blocks, dtype=jnp.int32) * BM
    fill_words = (dump_base + (jnp.arange(n_slots, dtype=jnp.int32) & (2 * BM - 1))) << ROW_BITS

    h2_all = jnp.zeros((t_all * SUB, LANES), F32)
    route_all = jnp.zeros((t_all, LANES), F32)

    pool_p, pool_s, v_s = [], [], []
    for l in range(DEPTH):
        lw = {
            "w_in": w_in[l].astype(BF16),
            "w_out": w_out[l].astype(BF16),
            "pool_w": pool_w[l].astype(BF16),
            "pool_scale": pool_scale[l].reshape(1, D_POOL),
            "sgu_g": sgu_norm_g[l].reshape(1, D_SGU),
            "sgu_w": sgu_w[l],
            "sgu_b_t": sgu_b[l].T,
            "sgu_w0": jnp.repeat(sgu_w[l, :, 0, 0], SGU_HEAD_DIM).reshape(1, D_SGU),
            "sgu_b0": jnp.repeat(sgu_b[l, :, 0], SGU_HEAD_DIM).reshape(1, D_SGU),
            "ln1_g": ln1_g[l].reshape(1, D_MODEL),
            "ln1_b": ln1_b[l].reshape(1, D_MODEL),
            "w_r": _split_bf16(jnp.concatenate([router_e_w[l], router_g_w[l], lane_pad], axis=1)),
            "b_r": jnp.concatenate([router_e_b[l], router_g_b[l],
                                    jnp.zeros((LANES - N_EXPERTS - N_GROUPS,), F32)]).reshape(1, LANES),
            "ltri_p": ltri_p,
            "ltri_s": ltri_s,
        }
        mod_p = mod[l, :n_seq].reshape(n_seq, 6, D_MODEL)
        mod_s = mod[l, n_seq:]

        x1_all, h2_all, route_all, nbuf_p, cnt_p = _mix_prompt_call(
            x_all, h2_all, route_all, mod_p, lw, n_seq, seq_len)
        hist = jnp.transpose(state_pool[l], (1, 0, 2))
        x1_all, h2_all, route_all, nbuf_s, vn_s, cnt = _mix_sample_call(
            x1_all, h2_all, route_all, mod_s, hist, lw, cnt_p, n_dec, t_p)
        pool_p.append(nbuf_p[:, HIST - POOL_BUF:])
        pool_s.append(jnp.transpose(nbuf_s, (1, 0, 2)))
        v_s.append(vn_s.reshape(n_dec, 1, D_SGU))

        counts = cnt[0, :N_EXPERTS].astype(jnp.int32)
        padded = ((counts + BM - 1) // BM) * BM
        pends = jnp.cumsum(padded)
        pstarts = pends - padded
        e_sel = route_all[:, 0:2].astype(jnp.int32)
        rank = route_all[:, 4:6].astype(jnp.int32)
        dest = rank + jnp.sum(jnp.where(e_sel[..., None] == expert_ids, pstarts, 0), axis=-1)
        blk_valid = block_start < pends[-1]
        n_used = (pends[-1:] // BM).astype(jnp.int32)
        blk_e = jnp.sum((pends[None, :] <= block_start[:, None]).astype(jnp.int32), axis=1)
        blk_e = jnp.minimum(blk_e, N_EXPERTS - 1)
        last_e = jnp.max(jnp.where(blk_valid, blk_e, 0))
        blk_e = jnp.where(blk_valid, blk_e, last_e).astype(jnp.int32)
        blk_first = (blk_valid & jnp.any(pstarts[None, :] == block_start[:, None], axis=1)
                     ).astype(jnp.int32)
        has_tokens = counts > 0
        later = (expert_ids[None, :] > expert_ids[:, None]) & has_tokens[None, :]
        next_e = jnp.min(jnp.where(later, expert_ids[None, :], N_EXPERTS), axis=1)
        next_e = jnp.where(next_e == N_EXPERTS, expert_ids, next_e)
        wslot = (jnp.cumsum(has_tokens.astype(jnp.int32)) - 1) & 1
        of_blk = blk_e[:, None] == expert_ids[None, :]
        blk_next_e = jnp.sum(jnp.where(of_blk, next_e[None, :], 0), axis=1).astype(jnp.int32)
        blk_wslot = jnp.sum(jnp.where(of_blk, wslot[None, :], 0), axis=1).astype(jnp.int32)

        xs, slots = _dispatch_call(pends, dest, h2_all, fill_words, n_blocks, n_assign // BM,
                                   choice_stride)
        y2 = _expert_call(l, blk_e, blk_first, n_used, blk_next_e, blk_wslot,
                          slots.reshape(n_blocks, 1, BM), xs,
                          exp_w_gate, exp_w_up, exp_w_down, t_all, choice_stride)

        ln2g = ln2_g[l].reshape(1, D_MODEL)
        ln2b = ln2_b[l].reshape(1, D_MODEL)
        tiles_per_seq = seq_len // TM_ROW
        g2_p_spec = pl.BlockSpec((None, 1, D_MODEL), lambda i: (i // tiles_per_seq, 0, 0))
        last = l == DEPTH - 1
        out_p = _combine_call(x1_all, route_all, y2, mod_p[:, 5:6, :], g2_p_spec, ln2g, ln2b,
                              TM_ROW, 0, t_p, choice_stride, in_place=not last)
        out_s = _combine_call(x1_all if last else out_p, route_all, y2, mod_s[:, 5 * D_MODEL:],
                              _full((n_dec, D_MODEL)), ln2g, ln2b,
                              n_dec, t_p, n_dec, choice_stride, in_place=not last)
        x_all = out_s

    return (out_p.reshape(n_seq, seq_len, D_MODEL),
            out_s.reshape(n_dec, 1, D_MODEL),
            jnp.stack(pool_p, axis=0),
            jnp.stack(pool_s, axis=0),
            jnp.stack(v_s, axis=0))
```

```python
import functools

import jax
import jax.numpy as jnp
from jax import lax
from jax.experimental import pallas as pl
from jax.experimental.pallas import tpu as pltpu

D_MODEL = 1024
DEPTH = 4
PAST_LEN = 16384
D_POOL = 512
N_POOL_GROUPS = 4
POOL_GROUP_DIM = 128
POOL_WINDOWS = (2, 4, 8, 16)
POOL_BUF = 15
HIST = 16
D_SGU = 512
SGU_HEADS = 4
SGU_HEAD_DIM = 128
CHUNK = 128
N_GROUPS = 4
EXPERTS_PER_GROUP = 8
N_EXPERTS = 32
D_EXPERT = 512
DEEPNORM_ALPHA = (2.0 * DEPTH) ** 0.25
LN_EPS = 1e-5

LANES = 128
SUB = 8
assert D_MODEL == SUB * LANES
ROUTE_G_LANE = N_EXPERTS
SECOND_LANE = 64

TM_MIX = 512
TM_ROW = 256
TM_DISPATCH = 1376
BM = 256
ROW_BITS = 16

F32 = jnp.float32
BF16 = jnp.bfloat16
VMEM_LIMIT = 48 * 1024 * 1024


def _layer_norm(r, g, b):
    mu = jnp.mean(r, axis=-1, keepdims=True)
    rc = r - mu
    var = jnp.mean(rc * rc, axis=-1, keepdims=True)
    return rc * lax.rsqrt(var + LN_EPS) * g + b


def _head_norm(vh, g):
    mu = jnp.mean(vh, axis=-1, keepdims=True)
    vc = vh - mu
    var = jnp.mean(vc * vc, axis=-1, keepdims=True)
    return vc * lax.rsqrt(var + LN_EPS) * g


def _roll_half(row):
    return pltpu.roll(jnp.broadcast_to(row, (8, LANES)), SECOND_LANE, 1)[0:1]


def _store_row_tiles(ref, val):
    rows = val.shape[0]
    for c in range(SUB):
        ref[pl.ds(c, rows, stride=SUB), :] = val[:, c * LANES:(c + 1) * LANES]


def _load_row_tiles(ref, rows):
    return jnp.concatenate(
        [ref[pl.ds(c, rows, stride=SUB), :] for c in range(SUB)], axis=1)


def _route(logits, ltri_ref, run_ref):
    tm = logits.shape[0]
    lane = lax.broadcasted_iota(jnp.int32, (tm, LANES), 1)
    lanef = lane.astype(F32)
    neg = -jnp.inf
    is_g = (lane >= ROUTE_G_LANE) & (lane < ROUTE_G_LANE + N_GROUPS)
    glm = jnp.where(is_g, logits, neg)
    gmax = jnp.max(glm, axis=1, keepdims=True)
    g_idx = jnp.min(jnp.where(glm == gmax, lanef - ROUTE_G_LANE, 1e4), axis=1, keepdims=True)
    p_g = 1.0 / jnp.sum(jnp.exp(glm - gmax), axis=1, keepdims=True)

    in_grp = (lane < N_EXPERTS) & ((lane >> 3).astype(F32) == g_idx)
    elm = jnp.where(in_grp, logits, neg)
    m1 = jnp.max(elm, axis=1, keepdims=True)
    i1 = jnp.min(jnp.where(elm == m1, lanef, 1e4), axis=1, keepdims=True)
    elm2 = jnp.where(lanef == i1, neg, elm)
    m2 = jnp.max(elm2, axis=1, keepdims=True)
    i2 = jnp.min(jnp.where(elm2 == m2, lanef, 1e4), axis=1, keepdims=True)
    e21 = jnp.exp(m2 - m1)
    den = 1.0 + e21
    w0 = (1.0 / den) * p_g
    w1 = (e21 / den) * p_g

    hit0 = lanef == i1
    hit1 = lanef == i2 + SECOND_LANE
    oh = jnp.where(hit0 | hit1, 1.0, 0.0)
    before = jnp.dot(ltri_ref[...], oh.astype(BF16), preferred_element_type=F32)
    tot = jnp.sum(oh, axis=0, keepdims=True)
    lane1 = lax.broadcasted_iota(jnp.int32, (1, LANES), 1)
    tot0 = jnp.where(lane1 < N_EXPERTS, tot, 0.0)
    tot1 = jnp.where(lane1 >= SECOND_LANE, tot, 0.0)
    run = run_ref[...]
    base = run + _roll_half(run + tot0)
    val = before + base
    rank0 = jnp.sum(jnp.where(hit0, val, 0.0), axis=1, keepdims=True)
    rank1 = jnp.sum(jnp.where(hit1, val, 0.0), axis=1, keepdims=True)
    run_ref[...] = run + tot0 + _roll_half(tot1)

    out = jnp.where(lane == 0, i1, 0.0)
    out = jnp.where(lane == 1, i2, out)
    out = jnp.where(lane == 2, w0, out)
    out = jnp.where(lane == 3, w1, out)
    out = jnp.where(lane == 4, rank0, out)
    out = jnp.where(lane == 5, rank1, out)
    return out


def _finish_mix(x, mix, g1, sh2, sc2, ln1g_ref, ln1b_ref, wr_ref, br_ref, ltri_ref, run_ref,
                x1_ref, h2_ref, route_ref):
    r = DEEPNORM_ALPHA * x + (1.0 + g1) * mix
    x1 = _layer_norm(r, ln1g_ref[...], ln1b_ref[...])
    x1_ref[...] = x1
    h2 = x1 * (1.0 + sc2) + sh2
    _store_row_tiles(h2_ref, h2)
    hi = h2.astype(BF16)
    lo = (h2 - hi.astype(F32)).astype(BF16)
    both = jnp.dot(hi, wr_ref[...], preferred_element_type=F32)
    logits = (both[:, :LANES] + both[:, LANES:]
              + jnp.dot(lo, wr_ref[:, :LANES], preferred_element_type=F32) + br_ref[...])
    route_ref[...] = _route(logits, ltri_ref, run_ref)


def _masked_sgu_w(sguw_ref, hd):
    t = lax.broadcasted_iota(jnp.int32, (CHUNK, CHUNK), 0)
    s = lax.broadcasted_iota(jnp.int32, (CHUNK, CHUNK), 1)
    return jnp.where(s <= t, sguw_ref[hd], 0.0).astype(BF16)


def _full(shape):
    nd = len(shape)
    return pl.BlockSpec(shape, lambda *_: (0,) * nd)


_ANY = pl.BlockSpec(memory_space=pl.ANY)


def _mod_kernel(c_ref, w_ref, b_ref, o_ref):
    s = jax.nn.silu(c_ref[...])
    o_ref[...] = jnp.dot(s.astype(BF16), w_ref[...].astype(BF16),
                         preferred_element_type=F32) + b_ref[...]


def _mod_call(c_all, w_ada, b_ada):
    n = c_all.shape[0]
    tn = 1024
    return pl.pallas_call(
        _mod_kernel,
        grid=(DEPTH, 6 * D_MODEL // tn),
        in_specs=[
            pl.BlockSpec((n, D_MODEL), lambda l, j: (0, 0)),
            pl.BlockSpec((None, D_MODEL, tn), lambda l, j: (l, 0, j)),
            pl.BlockSpec((None, 1, tn), lambda l, j: (l, 0, j)),
        ],
        out_specs=pl.BlockSpec((None, n, tn), lambda l, j: (l, 0, j)),
        out_shape=jax.ShapeDtypeStruct((DEPTH, n, 6 * D_MODEL), F32),
        name="adaln_mod",
        compiler_params=pltpu.CompilerParams(
            dimension_semantics=("arbitrary", "arbitrary")),
    )(c_all, w_ada, b_ada.reshape(DEPTH, 1, 6 * D_MODEL))


def _mix_prompt_kernel(x_ref, mod_ref, win_ref, wout_ref, poolw_ref, pscale_ref, sgug_ref,
                       sguw_ref, sgub_ref, ln1g_ref, ln1b_ref, wr_ref, br_ref, ltri_ref,
                       h2_in_ref, route_in_ref,
                       x1_ref, h2_ref, route_ref, nbuf_ref, cnt_ref,
                       pe_ref, mixin_ref, run_ref, *, tm, tiles_per_seq):
    del h2_in_ref, route_in_ref
    b = pl.program_id(0)
    j = pl.program_id(1)

    @pl.when((b == 0) & (j == 0))
    def _():
        run_ref[...] = jnp.zeros_like(run_ref)

    @pl.when(j == 0)
    def _():
        pe_ref[0:HIST, :] = jnp.zeros((HIST, D_POOL), F32)

    x = x_ref[...]
    mod = mod_ref[...]
    sh1, sc1, g1, sh2, sc2 = (mod[i:i + 1] for i in range(5))
    h = x * (1.0 + sc1) + sh1
    proj = jnp.dot(h.astype(BF16), win_ref[...], preferred_element_type=F32)
    p = proj[:, :D_POOL]
    u = proj[:, D_POOL:D_POOL + D_SGU]
    v = proj[:, D_POOL + D_SGU:]

    pe_ref[HIST:HIST + tm, :] = p
    pos = j * tm + lax.broadcasted_iota(jnp.int32, (tm, 1), 0)
    for g, w in enumerate(POOL_WINDOWS):
        lo, hi = g * POOL_GROUP_DIM, (g + 1) * POOL_GROUP_DIM
        pg = p[:, lo:hi]
        s = pg
        for k in range(1, w):
            s = s + pe_ref[pl.ds(HIST - k, tm), lo:hi]
        cnt = jnp.minimum(pos + 1, w).astype(F32)
        d = s / cnt - pg
        a = jnp.dot(d.astype(BF16), poolw_ref[g], preferred_element_type=F32)
        mixin_ref[:, lo:hi] = (a * pscale_ref[:, lo:hi]).astype(BF16)

    @pl.when(j == tiles_per_seq - 1)
    def _():
        nbuf_ref[...] = pe_ref[tm:tm + HIST, :]

    pe_ref[0:HIST, :] = pe_ref[tm:tm + HIST, :]

    for hd in range(SGU_HEADS):
        lo, hi = hd * SGU_HEAD_DIM, (hd + 1) * SGU_HEAD_DIM
        vn = _head_norm(v[:, lo:hi], sgug_ref[:, lo:hi]).astype(BF16)
        ws = _masked_sgu_w(sguw_ref, hd)
        bcol = sgub_ref[:, hd:hd + 1]
        for c in range(tm // CHUNK):
            r0, r1 = c * CHUNK, (c + 1) * CHUNK
            z = jnp.dot(ws, vn[r0:r1], preferred_element_type=F32) + bcol
            mixin_ref[r0:r1, D_POOL + lo:D_POOL + hi] = (u[r0:r1, lo:hi] * z).astype(BF16)

    mix = jnp.dot(mixin_ref[...], wout_ref[...], preferred_element_type=F32)
    _finish_mix(x, mix, g1, sh2, sc2, ln1g_ref, ln1b_ref, wr_ref, br_ref, ltri_ref, run_ref,
                x1_ref, h2_ref, route_ref)
    cnt_ref[...] = run_ref[...]


def _mix_prompt_call(x_all, h2_buf, route_buf, mod_p, lw, n_seq, seq_len):
    tm = TM_MIX
    tps = seq_len // tm
    t_all = x_all.shape[0]
    row = lambda b, j: (b * tps + j, 0)
    return pl.pallas_call(
        functools.partial(_mix_prompt_kernel, tm=tm, tiles_per_seq=tps),
        grid=(n_seq, tps),
        in_specs=[
            pl.BlockSpec((tm, D_MODEL), row),
            pl.BlockSpec((None, 6, D_MODEL), lambda b, j: (b, 0, 0)),
            _full((D_MODEL, D_POOL + 2 * D_SGU)),
            _full((D_POOL + D_SGU, D_MODEL)),
            _full((N_POOL_GROUPS, POOL_GROUP_DIM, POOL_GROUP_DIM)),
            _full((1, D_POOL)),
            _full((1, D_SGU)),
            _full((SGU_HEADS, CHUNK, CHUNK)),
            _full((CHUNK, SGU_HEADS)),
            _full((1, D_MODEL)),
            _full((1, D_MODEL)),
            _full((D_MODEL, 2 * LANES)),
            _full((1, LANES)),
            _full((tm, tm)),
            _ANY, _ANY,
        ],
        out_specs=[
            pl.BlockSpec((tm, D_MODEL), row),
            pl.BlockSpec((tm * SUB, LANES), row),
            pl.BlockSpec((tm, LANES), row),
            pl.BlockSpec((None, HIST, D_POOL), lambda b, j: (b, 0, 0)),
            pl.BlockSpec((1, LANES), lambda b, j: (0, 0)),
        ],
        out_shape=[
            jax.ShapeDtypeStruct((t_all, D_MODEL), F32),
            jax.ShapeDtypeStruct((t_all * SUB, LANES), F32),
            jax.ShapeDtypeStruct((t_all, LANES), F32),
            jax.ShapeDtypeStruct((n_seq, HIST, D_POOL), F32),
            jax.ShapeDtypeStruct((1, LANES), F32),
        ],
        scratch_shapes=[
            pltpu.VMEM((HIST + tm, D_POOL), F32),
            pltpu.VMEM((tm, D_POOL + D_SGU), BF16),
            pltpu.VMEM((1, LANES), F32),
        ],
        input_output_aliases={0: 0, 14: 1, 15: 2},
        name="mix_prompt",
        compiler_params=pltpu.CompilerParams(
            dimension_semantics=("arbitrary", "arbitrary"),
            vmem_limit_bytes=VMEM_LIMIT),
    )(x_all, mod_p, lw["w_in"], lw["w_out"], lw["pool_w"], lw["pool_scale"], lw["sgu_g"],
      lw["sgu_w"], lw["sgu_b_t"], lw["ln1_g"], lw["ln1_b"], lw["w_r"], lw["b_r"], lw["ltri_p"],
      h2_buf, route_buf)


def _mix_sample_kernel(x_ref, mod_ref, hist_ref, win_ref, wout_ref, poolw_ref, pscale_ref,
                       sgug_ref, sguw0_ref, sgub0_ref, ln1g_ref, ln1b_ref, wr_ref, br_ref,
                       ltri_ref, cnt_in_ref, h2_in_ref, route_in_ref,
                       x1_ref, h2_ref, route_ref, nbuf_ref, vn_ref, cnt_ref,
                       mixin_ref, run_ref):
    del h2_in_ref, route_in_ref
    run_ref[...] = cnt_in_ref[...]
    x = x_ref[...]
    sh1, sc1, g1, sh2, sc2 = (mod_ref[:, i * D_MODEL:(i + 1) * D_MODEL] for i in range(5))
    h = x * (1.0 + sc1) + sh1
    proj = jnp.dot(h.astype(BF16), win_ref[...], preferred_element_type=F32)
    p = proj[:, :D_POOL]
    u = proj[:, D_POOL:D_POOL + D_SGU]
    v = proj[:, D_POOL + D_SGU:]

    for g, w in enumerate(POOL_WINDOWS):
        lo, hi = g * POOL_GROUP_DIM, (g + 1) * POOL_GROUP_DIM
        pg = p[:, lo:hi]
        s = pg
        for k in range(1, w):
            s = s + hist_ref[POOL_BUF - k, :, lo:hi]
        cnt = float(min(PAST_LEN + 1, w))
        d = s / cnt - pg
        a = jnp.dot(d.astype(BF16), poolw_ref[g], preferred_element_type=F32)
        mixin_ref[:, lo:hi] = (a * pscale_ref[:, lo:hi]).astype(BF16)

    for k in range(POOL_BUF - 1):
        nbuf_ref[k] = hist_ref[k + 1]
    nbuf_ref[POOL_BUF - 1] = p

    for hd in range(SGU_HEADS):
        lo, hi = hd * SGU_HEAD_DIM, (hd + 1) * SGU_HEAD_DIM
        vn = _head_norm(v[:, lo:hi], sgug_ref[:, lo:hi])
        vn_ref[:, lo:hi] = vn
        z = sguw0_ref[:, lo:hi].astype(BF16).astype(F32) * vn.astype(BF16).astype(F32) \
            + sgub0_ref[:, lo:hi]
        mixin_ref[:, D_POOL + lo:D_POOL + hi] = (u[:, lo:hi] * z).astype(BF16)

    mix = jnp.dot(mixin_ref[...], wout_ref[...], preferred_element_type=F32)
    _finish_mix(x, mix, g1, sh2, sc2, ln1g_ref, ln1b_ref, wr_ref, br_ref, ltri_ref, run_ref,
                x1_ref, h2_ref, route_ref)
    cnt_ref[...] = run_ref[...]


def _mix_sample_call(x1_all, h2_all, route_all, mod_s, hist, lw, cnt_in, n_dec, row0):
    blk = row0 // n_dec
    rows = lambda r, w: pl.BlockSpec((r, w), lambda i: (blk, 0))
    return pl.pallas_call(
        _mix_sample_kernel,
        grid=(1,),
        in_specs=[
            rows(n_dec, D_MODEL),
            _full((n_dec, 6 * D_MODEL)),
            _full((POOL_BUF, n_dec, D_POOL)),
            _full((D_MODEL, D_POOL + 2 * D_SGU)),
            _full((D_POOL + D_SGU, D_MODEL)),
            _full((N_POOL_GROUPS, POOL_GROUP_DIM, POOL_GROUP_DIM)),
            _full((1, D_POOL)),
            _full((1, D_SGU)),
            _full((1, D_SGU)),
            _full((1, D_SGU)),
            _full((1, D_MODEL)),
            _full((1, D_MODEL)),
            _full((D_MODEL, 2 * LANES)),
            _full((1, LANES)),
            _full((n_dec, n_dec)),
            _full((1, LANES)),
            _ANY, _ANY,
        ],
        out_specs=[
            rows(n_dec, D_MODEL),
            rows(n_dec * SUB, LANES),
            rows(n_dec, LANES),
            _full((POOL_BUF, n_dec, D_POOL)),
            _full((n_dec, D_SGU)),
            _full((1, LANES)),
        ],
        out_shape=[
            jax.ShapeDtypeStruct(x1_all.shape, F32),
            jax.ShapeDtypeStruct(h2_all.shape, F32),
            jax.ShapeDtypeStruct(route_all.shape, F32),
            jax.ShapeDtypeStruct((POOL_BUF, n_dec, D_POOL), F32),
            jax.ShapeDtypeStruct((n_dec, D_SGU), F32),
            jax.ShapeDtypeStruct((1, LANES), F32),
        ],
        scratch_shapes=[
            pltpu.VMEM((n_dec, D_POOL + D_SGU), BF16),
            pltpu.VMEM((1, LANES), F32),
        ],
        input_output_aliases={0: 0, 16: 1, 17: 2},
        name="mix_sample",
        compiler_params=pltpu.CompilerParams(
            dimension_semantics=("arbitrary",),
            vmem_limit_bytes=VMEM_LIMIT),
    )(x1_all, mod_s, hist, lw["w_in"], lw["w_out"], lw["pool_w"], lw["pool_scale"], lw["sgu_g"],
      lw["sgu_w0"], lw["sgu_b0"], lw["ln1_g"], lw["ln1_b"], lw["w_r"], lw["b_r"], lw["ltri_s"],
      cnt_in, h2_all, route_all)


def _tile_copy(src_ref, src_row, dst_ref, dst_row, sem):
    def tile(ref, row):
        start = row * SUB if isinstance(row, int) else pl.multiple_of(row * SUB, SUB)
        return ref.at[pl.ds(start, SUB), :]

    return pltpu.make_async_copy(tile(src_ref, src_row), tile(dst_ref, dst_row), sem)


def _dispatch_kernel(pends_ref, dest_ref, h2_ref, fill_ref, xs_ref, slots_ref, zero_ref, sem,
                     *, tm, n_blocks, min_used, choice_stride):
    i = pl.program_id(0)

    @pl.when(i == 0)
    def _():
        fill = pltpu.make_async_copy(fill_ref, slots_ref, sem)
        fill.start()
        fill.wait()
        zero_ref[...] = jnp.zeros_like(zero_ref)
        n_used = lax.shift_right_logical(pends_ref[N_EXPERTS - 1], BM.bit_length() - 1)

        def block_copy(b):
            first = pl.multiple_of(b * (BM * SUB), BM * SUB)
            return pltpu.make_async_copy(zero_ref, xs_ref.at[pl.ds(first, BM * SUB), :], sem)

        def last_block(e):
            return lax.shift_right_logical(pends_ref[e], BM.bit_length() - 1) - 1

        for act in ("start", "wait"):
            for e in range(N_EXPERTS):
                prev_end = pends_ref[e - 1] if e else 0
                pl.when(pends_ref[e] > prev_end)(
                    lambda e=e, act=act: getattr(block_copy(last_block(e)), act)())
            for b in range(min_used, n_blocks):
                pl.when(b >= n_used)(lambda b=b, act=act: getattr(block_copy(b), act)())

    tile_word = (i * tm) << ROW_BITS
    for r in range(tm):
        for k in range(2):
            slot = dest_ref[0, 2 * r + k]
            _tile_copy(h2_ref, r, xs_ref, slot, sem).start(priority=k)
            slots_ref[slot] = tile_word + ((k * choice_stride + r) << ROW_BITS)
    for r in range(tm):
        for k in range(2):
            _tile_copy(h2_ref, r, xs_ref, 0, sem).wait()


def _dispatch_call(pends, dest, h2_all, fill_words, n_blocks, min_used, choice_stride):
    tm = TM_DISPATCH
    n_tiles = dest.shape[0] // tm
    grid_spec = pltpu.PrefetchScalarGridSpec(
        num_scalar_prefetch=1,
        grid=(n_tiles,),
        in_specs=[
            pl.BlockSpec((None, 1, 2 * tm), lambda i, pe: (i, 0, 0), memory_space=pltpu.SMEM),
            pl.BlockSpec((tm * SUB, LANES), lambda i, pe: (i, 0)),
            _ANY,
        ],
        out_specs=[_ANY, pl.BlockSpec(memory_space=pltpu.SMEM)],
        scratch_shapes=[pltpu.VMEM((BM * SUB, LANES), F32), pltpu.SemaphoreType.DMA],
    )
    return pl.pallas_call(
        functools.partial(_dispatch_kernel, tm=tm, n_blocks=n_blocks, min_used=min_used,
                          choice_stride=choice_stride),
        grid_spec=grid_spec,
        out_shape=[jax.ShapeDtypeStruct((n_blocks * BM * SUB, LANES), F32),
                   jax.ShapeDtypeStruct(fill_words.shape, jnp.int32)],
        name="dispatch_rows",
        compiler_params=pltpu.CompilerParams(dimension_semantics=("arbitrary",)),
    )(pends, dest.reshape(n_tiles, 1, 2 * tm), h2_all, fill_words)


SCATTER_GROUPS = 4


def _expert_kernel(be_ref, bfirst_ref, nused_ref, bnext_ref, bslot_ref, slot_prev_ref, slot_ref,
                   xs_ref, wg_ref, wu_ref, wd_ref,
                   y2_ref, ybuf_ref, wgb_ref, wub_ref, wdb_ref, wgf_ref, wuf_ref, wdf_ref,
                   ssem, wsem, *, layer, dump_base, unowned_rows):
    i = pl.program_id(0)
    n_used = nused_ref[0]

    def start_scatter(words_ref, buf, rows=range(BM)):
        for r in rows:
            row = lax.shift_right_logical(words_ref[0, r], ROW_BITS)
            _tile_copy(ybuf_ref.at[buf], r, y2_ref, row, ssem.at[buf]).start(priority=r % 2)

    def wait_scatter(buf):
        for r in range(BM):
            _tile_copy(ybuf_ref.at[buf], r, y2_ref, 0, ssem.at[buf]).wait()

    @pl.when(i == 0)
    def _():
        ybuf_ref[...] = jnp.zeros_like(ybuf_ref)
        unowned = [row for lo, hi in unowned_rows for row in range(lo, hi)]
        assert len(unowned) <= BM
        for r, row in enumerate(unowned):
            _tile_copy(ybuf_ref.at[1], r, y2_ref, row, ssem.at[1]).start(priority=r % 2)
        for r, row in enumerate(unowned):
            _tile_copy(ybuf_ref.at[1], r, y2_ref, row, ssem.at[1]).wait()
        for buf in range(2):
            for r in range(BM):
                _tile_copy(ybuf_ref.at[buf], r, y2_ref, dump_base + buf * BM + r,
                           ssem.at[buf]).start(priority=r % 2)

    def weight_copies(e, slot):
        return [pltpu.make_async_copy(src.at[layer, e], dst.at[slot], wsem.at[slot])
                for src, dst in ((wg_ref, wgf_ref), (wu_ref, wuf_ref), (wd_ref, wdf_ref))]

    @pl.when(i == 0)
    def _():
        for c in weight_copies(be_ref[0], 0):
            c.start()

    @pl.when(bfirst_ref[i] == 1)
    def _():
        e = be_ref[i]
        slot = bslot_ref[i]
        for c in weight_copies(e, slot):
            c.wait()
        wgb_ref[...] = wgf_ref[slot].astype(BF16)
        wub_ref[...] = wuf_ref[slot].astype(BF16)
        wdb_ref[...] = wdf_ref[slot].astype(BF16)

        @pl.when(bnext_ref[i] != e)
        def _():
            for c in weight_copies(bnext_ref[i], 1 - slot):
                c.start()

    def step(buf, scatter_prev):
        per_group = BM // SCATTER_GROUPS
        groups = iter(range(g * per_group, (g + 1) * per_group) for g in range(SCATTER_GROUPS))

        def scatter_piece():
            if scatter_prev:
                start_scatter(slot_prev_ref, 1 - buf, next(groups))

        x = _load_row_tiles(xs_ref, BM).astype(BF16)
        half = D_EXPERT // 2
        acts = []
        for lo in (0, half):
            g = jnp.dot(x, wgb_ref[:, lo:lo + half], preferred_element_type=F32)
            u = jnp.dot(x, wub_ref[:, lo:lo + half], preferred_element_type=F32)
            acts.append((jax.nn.silu(g) * u).astype(BF16))
            scatter_piece()
        a = jnp.concatenate(acts, axis=1)
        half = D_MODEL // 2
        ys = []
        for lo in (0, half):
            ys.append(jnp.dot(a, wdb_ref[:, lo:lo + half], preferred_element_type=F32))
            scatter_piece()
        y = jnp.concatenate(ys, axis=1)
        wait_scatter(buf)
        _store_row_tiles(ybuf_ref.at[buf], y)

        @pl.when(i == n_used - 1)
        def _():
            start_scatter(slot_ref, buf)
            wait_scatter(1 - buf)
            wait_scatter(buf)

    pl.when(i == 0)(functools.partial(step, 0, False))
    for buf in range(2):
        pl.when((i > 0) & (i < n_used) & (i % 2 == buf))(functools.partial(step, buf, True))


def _expert_call(layer, blk_e, blk_first, n_used, blk_next_e, blk_wslot, slots, xs,
                 w_gate, w_up, w_down, t_all, choice_stride):
    n_blocks = blk_e.shape[0]
    dump_base = 2 * choice_stride
    n_y_rows = dump_base + 2 * BM
    unowned_rows = ((t_all, choice_stride), (choice_stride + t_all, 2 * choice_stride))
    used = lambda i, nu: jnp.minimum(i, nu[0] - 1)
    grid_spec = pltpu.PrefetchScalarGridSpec(
        num_scalar_prefetch=5,
        grid=(n_blocks,),
        in_specs=[
            pl.BlockSpec((None, 1, BM),
                         lambda i, be, bf, nu, bn, bs: (jnp.clip(i - 1, 0, nu[0] - 1), 0, 0),
                         memory_space=pltpu.SMEM),
            pl.BlockSpec((None, 1, BM), lambda i, be, bf, nu, bn, bs: (used(i, nu), 0, 0),
                         memory_space=pltpu.SMEM),
            pl.BlockSpec((BM * SUB, LANES), lambda i, be, bf, nu, bn, bs: (used(i, nu), 0)),
            _ANY, _ANY, _ANY,
        ],
        out_specs=_ANY,
        scratch_shapes=[
            pltpu.VMEM((2, BM * SUB, LANES), F32),
            pltpu.VMEM((D_MODEL, D_EXPERT), BF16),
            pltpu.VMEM((D_MODEL, D_EXPERT), BF16),
            pltpu.VMEM((D_EXPERT, D_MODEL), BF16),
            pltpu.VMEM((2, D_MODEL, D_EXPERT), F32),
            pltpu.VMEM((2, D_MODEL, D_EXPERT), F32),
            pltpu.VMEM((2, D_EXPERT, D_MODEL), F32),
            pltpu.SemaphoreType.DMA((2,)),
            pltpu.SemaphoreType.DMA((2,)),
        ],
    )
    return pl.pallas_call(
        functools.partial(_expert_kernel, layer=layer, dump_base=dump_base,
                          unowned_rows=unowned_rows),
        grid_spec=grid_spec,
        out_shape=jax.ShapeDtypeStruct((n_y_rows * SUB, LANES), F32),
        name="expert_mlp",
        compiler_params=pltpu.CompilerParams(
            dimension_semantics=("arbitrary",),
            vmem_limit_bytes=VMEM_LIMIT),
    )(blk_e, blk_first, n_used, blk_next_e, blk_wslot, slots, slots, xs, w_gate, w_up, w_down)


def _combine_kernel(x1_ref, route_ref, y0_ref, y1_ref, g2_ref, ln2g_ref, ln2b_ref, out_ref, *, tm):
    route = route_ref[...]
    f = route[:, 2:3] * _load_row_tiles(y0_ref, tm) + route[:, 3:4] * _load_row_tiles(y1_ref, tm)
    r = DEEPNORM_ALPHA * x1_ref[...] + (1.0 + g2_ref[...]) * f
    out_ref[...] = _layer_norm(r, ln2g_ref[...], ln2b_ref[...])


def _combine_call(x1_all, route_all, y2, g2, g2_spec, ln2_g, ln2_b, tm, row0, n_rows,
                  choice_stride, in_place):
    blk0 = row0 // tm
    blk1 = (choice_stride + row0) // tm
    assert row0 % tm == 0 and n_rows % tm == 0 and choice_stride % tm == 0
    rows = lambda i: (blk0 + i, 0)
    if in_place:
        out_spec = pl.BlockSpec((tm, D_MODEL), rows)
        out_shape = jax.ShapeDtypeStruct(x1_all.shape, F32)
        aliases = {0: 0}
    else:
        out_spec = pl.BlockSpec((tm, D_MODEL), lambda i: (i, 0))
        out_shape = jax.ShapeDtypeStruct((n_rows, D_MODEL), F32)
        aliases = {}
    return pl.pallas_call(
        functools.partial(_combine_kernel, tm=tm),
        grid=(n_rows // tm,),
        in_specs=[
            pl.BlockSpec((tm, D_MODEL), rows),
            pl.BlockSpec((tm, LANES), rows),
            pl.BlockSpec((tm * SUB, LANES), rows),
            pl.BlockSpec((tm * SUB, LANES), lambda i: (blk1 + i, 0)),
            g2_spec,
            _full((1, D_MODEL)),
            _full((1, D_MODEL)),
        ],
        out_specs=out_spec,
        out_shape=out_shape,
        input_output_aliases=aliases,
        name="combine_ln2",
        compiler_params=pltpu.CompilerParams(
            dimension_semantics=("arbitrary",),
            vmem_limit_bytes=VMEM_LIMIT),
    )(x1_all, route_all, y2, y2, g2, ln2_g, ln2_b)


def _split_bf16(w):
    hi = w.astype(BF16)
    lo = (w - hi.astype(F32)).astype(BF16)
    return jnp.concatenate([hi, lo], axis=1)


def _strict_lower(n):
    r = lax.broadcasted_iota(jnp.int32, (n, n), 0)
    c = lax.broadcasted_iota(jnp.int32, (n, n), 1)
    return (c < r).astype(BF16)


def kernel(x_prompt, x_sample, state_pool, c_prompt, c_sample, w_ada, b_ada, w_in, pool_w, pool_scale, sgu_norm_g, sgu_w, sgu_b, w_out, ln1_g, ln1_b, router_g_w, router_g_b, router_e_w, router_e_b, exp_w_gate, exp_w_up, exp_w_down, ln2_g, ln2_b):
    n_seq, seq_len, _ = x_prompt.shape
    n_dec = x_sample.shape[0]
    t_p = n_seq * seq_len
    t_all = t_p + n_dec
    assert t_p % n_dec == 0 and seq_len % TM_MIX == 0 and seq_len % TM_ROW == 0
    assert t_all % TM_DISPATCH == 0
    n_assign = 2 * t_all
    n_blocks = -(-n_assign // BM) + N_EXPERTS
    n_slots = n_blocks * BM
    choice_stride = -(-t_all // TM_ROW) * TM_ROW
    dump_base = 2 * choice_stride
    n_y_rows = dump_base + 2 * BM
    assert n_y_rows <= (1 << (32 - ROW_BITS)) and t_all <= (1 << ROW_BITS)

    x_all = jnp.concatenate([x_prompt.reshape(t_p, D_MODEL), x_sample.reshape(n_dec, D_MODEL)])
    mod = _mod_call(jnp.concatenate([c_prompt, c_sample], axis=0), w_ada, b_ada)
    ltri_p = _strict_lower(TM_MIX)
    ltri_s = _strict_lower(n_dec)
    lane_pad = jnp.zeros((D_MODEL, LANES - N_EXPERTS - N_GROUPS), F32)
    expert_ids = jnp.arange(N_EXPERTS, dtype=jnp.int32)
    block_start = jnp.arange(n_blocks, dtype=jnp.int32) * BM
    fill_words = (dump_base + (jnp.arange(n_slots, dtype=jnp.int32) & (2 * BM - 1))) << ROW_BITS

    h2_all = jnp.zeros((t_all * SUB, LANES), F32)
    route_all = jnp.zeros((t_all, LANES), F32)

    pool_p, pool_s, v_s = [], [], []
    for l in range(DEPTH):
        lw = {
            "w_in": w_in[l].astype(BF16),
            "w_out": w_out[l].astype(BF16),
            "pool_w": pool_w[l].astype(BF16),
            "pool_scale": pool_scale[l].reshape(1, D_POOL),
            "sgu_g": sgu_norm_g[l].reshape(1, D_SGU),
            "sgu_w": sgu_w[l],
            "sgu_b_t": sgu_b[l].T,
            "sgu_w0": jnp.repeat(sgu_w[l, :, 0, 0], SGU_HEAD_DIM).reshape(1, D_SGU),
            "sgu_b0": jnp.repeat(sgu_b[l, :, 0], SGU_HEAD_DIM).reshape(1, D_SGU),
            "ln1_g": ln1_g[l].reshape(1, D_MODEL),
            "ln1_b": ln1_b[l].reshape(1, D_MODEL),
            "w_r": _split_bf16(jnp.concatenate([router_e_w[l], router_g_w[l], lane_pad], axis=1)),
            "b_r": jnp.concatenate([router_e_b[l], router_g_b[l],
                                    jnp.zeros((LANES - N_EXPERTS - N_GROUPS,), F32)]).reshape(1, LANES),
            "ltri_p": ltri_p,
            "ltri_s": ltri_s,
        }
        mod_p = mod[l, :n_seq].reshape(n_seq, 6, D_MODEL)
        mod_s = mod[l, n_seq:]

        x1_all, h2_all, route_all, nbuf_p, cnt_p = _mix_prompt_call(
            x_all, h2_all, route_all, mod_p, lw, n_seq, seq_len)
        hist = jnp.transpose(state_pool[l], (1, 0, 2))
        x1_all, h2_all, route_all, nbuf_s, vn_s, cnt = _mix_sample_call(
            x1_all, h2_all, route_all, mod_s, hist, lw, cnt_p, n_dec, t_p)
        pool_p.append(nbuf_p[:, HIST - POOL_BUF:])
        pool_s.append(jnp.transpose(nbuf_s, (1, 0, 2)))
        v_s.append(vn_s.reshape(n_dec, 1, D_SGU))

        counts = cnt[0, :N_EXPERTS].astype(jnp.int32)
        padded = ((counts + BM - 1) // BM) * BM
        pends = jnp.cumsum(padded)
        pstarts = pends - padded
        e_sel = route_all[:, 0:2].astype(jnp.int32)
        rank = route_all[:, 4:6].astype(jnp.int32)
        dest = rank + jnp.sum(jnp.where(e_sel[..., None] == expert_ids, pstarts, 0), axis=-1)
        blk_valid = block_start < pends[-1]
        n_used = (pends[-1:] // BM).astype(jnp.int32)
        blk_e = jnp.sum((pends[None, :] <= block_start[:, None]).astype(jnp.int32), axis=1)
        blk_e = jnp.minimum(blk_e, N_EXPERTS - 1)
        last_e = jnp.max(jnp.where(blk_valid, blk_e, 0))
        blk_e = jnp.where(blk_valid, blk_e, last_e).astype(jnp.int32)
        blk_first = (blk_valid & jnp.any(pstarts[None, :] == block_start[:, None], axis=1)
                     ).astype(jnp.int32)
        has_tokens = counts > 0
        later = (expert_ids[None, :] > expert_ids[:, None]) & has_tokens[None, :]
        next_e = jnp.min(jnp.where(later, expert_ids[None, :], N_EXPERTS), axis=1)
        next_e = jnp.where(next_e == N_EXPERTS, expert_ids, next_e)
        wslot = (jnp.cumsum(has_tokens.astype(jnp.int32)) - 1) & 1
        of_blk = blk_e[:, None] == expert_ids[None, :]
        blk_next_e = jnp.sum(jnp.where(of_blk, next_e[None, :], 0), axis=1).astype(jnp.int32)
        blk_wslot = jnp.sum(jnp.where(of_blk, wslot[None, :], 0), axis=1).astype(jnp.int32)

        xs, slots = _dispatch_call(pends, dest, h2_all, fill_words, n_blocks, n_assign // BM,
                                   choice_stride)
        y2 = _expert_call(l, blk_e, blk_first, n_used, blk_next_e, blk_wslot,
                          slots.reshape(n_blocks, 1, BM), xs,
                          exp_w_gate, exp_w_up, exp_w_down, t_all, choice_stride)

        ln2g = ln2_g[l].reshape(1, D_MODEL)
        ln2b = ln2_b[l].reshape(1, D_MODEL)
        tiles_per_seq = seq_len // TM_ROW
        g2_p_spec = pl.BlockSpec((None, 1, D_MODEL), lambda i: (i // tiles_per_seq, 0, 0))
        last = l == DEPTH - 1
        out_p = _combine_call(x1_all, route_all, y2, mod_p[:, 5:6, :], g2_p_spec, ln2g, ln2b,
                              TM_ROW, 0, t_p, choice_stride, in_place=not last)
        out_s = _combine_call(x1_all if last else out_p, route_all, y2, mod_s[:, 5 * D_MODEL:],
                              _full((n_dec, D_MODEL)), ln2g, ln2b,
                              n_dec, t_p, n_dec, choice_stride, in_place=not last)
        x_all = out_s

    return (out_p.reshape(n_seq, seq_len, D_MODEL),
            out_s.reshape(n_dec, 1, D_MODEL),
            jnp.stack(pool_p, axis=0),
            jnp.stack(pool_s, axis=0),
            jnp.stack(v_s, axis=0))
```
